```python
import jax, jax.numpy as jnp
from jax import lax
import numpy as np

D_MODEL = 1024
BATCH = 8
SEQ = 2048
DEPTH = 2

ATTN_WIDTH = D_MODEL // 2
ATTN_HEAD_DIM = 64
ATTN_HEADS = ATTN_WIDTH // ATTN_HEAD_DIM
MOBA_BLOCK = 256
MOBA_TOPK = 3
Q_CHUNK = 128
SSD_INNER = D_MODEL // 2
SSD_HEAD_DIM = 64
SSD_HEADS = SSD_INNER // SSD_HEAD_DIM
SSD_GROUPS = 2
SSD_STATE = 128
SSD_CONV = 4
SSD_CHUNK = 256
SSD_CONV_CH = SSD_INNER + 2 * SSD_GROUPS * SSD_STATE
IN_COLS = 3 * ATTN_WIDTH + SSD_INNER + SSD_CONV_CH + SSD_HEADS
MIX_WIDTH = ATTN_WIDTH + SSD_INNER
CONF_KERNEL = 31
FFN_HIDDEN = -(-8 * D_MODEL // (3 * 256)) * 256
N_EVEN = (DEPTH + 1) // 2
N_ODD = DEPTH // 2
DN_ALPHA = (2 * DEPTH) ** 0.25
DN_BETA = (8 * DEPTH) ** -0.25
LN_EPS = 1e-5
RMS_EPS = 1e-5

kernel_name = 'moba_mamba2_conformer_deepnorm_hybrid'


def layer_norm(x, g, b):
    xf = x.astype(jnp.float32)
    mu = xf.mean(-1, keepdims=True)
    var = jnp.square(xf - mu).mean(-1, keepdims=True)
    y = (xf - mu) * lax.rsqrt(var + LN_EPS) * g.astype(jnp.float32) + b.astype(jnp.float32)
    return y.astype(x.dtype)


def pad_seq(x, mult, axis=1):
    p = (-x.shape[axis]) % mult
    if p == 0:
        return x
    pads = [(0, 0)] * x.ndim
    pads[axis] = (0, p)
    return jnp.pad(x, pads)


def causal_depthwise_conv(x, w, b):
    k_w, ch = w.shape
    y = lax.conv_general_dilated(x, w[:, None, :].astype(x.dtype), window_strides=(1,),
                                 padding=[(k_w - 1, 0)],
                                 dimension_numbers=('NWC', 'WIO', 'NWC'),
                                 feature_group_count=ch)
    return y + b


def alibi_slopes(n):
    return 2.0 ** (-8.0 * jnp.arange(1, n + 1, dtype=jnp.float32) / n)


def moba_attention(q, k, v):
    bsz, seq, heads, dh = q.shape
    n_blk = -(-seq // MOBA_BLOCK)
    n_sel = min(MOBA_TOPK, max(n_blk - 1, 1))
    n_qc = seq // Q_CHUNK
    k_blk = pad_seq(k, MOBA_BLOCK).transpose(0, 2, 1, 3).reshape(bsz, heads, n_blk, MOBA_BLOCK, dh)
    v_blk = pad_seq(v, MOBA_BLOCK).transpose(0, 2, 1, 3).reshape(bsz, heads, n_blk, MOBA_BLOCK, dh)
    k_mean = k_blk.mean(axis=3)
    q_ch = q.transpose(0, 2, 1, 3).reshape(bsz, heads, n_qc, Q_CHUNK, dh)
    slopes = alibi_slopes(heads)
    scale = dh ** -0.5
    h_idx = jnp.arange(heads)[:, None, None]
    blk_ids = jnp.arange(n_blk)
    in_blk = jnp.arange(MOBA_BLOCK)

    def one_chunk(n):
        b = n // n_qc
        c = n % n_qc
        qi = q_ch[b, :, c]
        kb, vb, km = k_blk[b], v_blk[b], k_mean[b]
        q_start = c * Q_CHUNK
        own = q_start // MOBA_BLOCK
        q_pos = q_start + jnp.arange(Q_CHUNK)
        gate = jnp.einsum('hqd,hnd->hqn', qi, km).astype(jnp.float32)
        gate = jnp.where(blk_ids < own, gate, -jnp.inf)
        _, sel = lax.top_k(gate, n_sel)
        valid = sel < own
        k_sel = kb[h_idx, sel]
        v_sel = vb[h_idx, sel]
        s_sel = jnp.einsum('hqd,hqkjd->hqkj', qi, k_sel).astype(jnp.float32) * scale
        dist_sel = (q_pos[None, :, None, None] - (sel[..., None] * MOBA_BLOCK + in_blk)).astype(jnp.float32)
        s_sel = jnp.where(valid[..., None], s_sel - slopes[:, None, None, None] * dist_sel, -jnp.inf)
        k_own = kb[:, own]
        v_own = vb[:, own]
        s_own = jnp.einsum('hqd,hjd->hqj', qi, k_own).astype(jnp.float32) * scale
        dist_own = q_pos[:, None] - (own * MOBA_BLOCK + in_blk)[None, :]
        s_own = jnp.where(dist_own >= 0,
                          s_own - slopes[:, None, None] * dist_own.astype(jnp.float32), -jnp.inf)
        scores = jnp.concatenate([s_sel.reshape(heads, Q_CHUNK, n_sel * MOBA_BLOCK), s_own], axis=-1)
        p = jax.nn.softmax(scores, axis=-1).astype(v.dtype)
        p_sel = p[..., :n_sel * MOBA_BLOCK].reshape(heads, Q_CHUNK, n_sel, MOBA_BLOCK)
        p_own = p[..., n_sel * MOBA_BLOCK:]
        return (jnp.einsum('hqkj,hqkjd->hqd', p_sel, v_sel)
                + jnp.einsum('hqj,hjd->hqd', p_own, v_own))

    out = lax.map(one_chunk, jnp.arange(bsz * n_qc))
    return out.reshape(bsz, n_qc, heads, Q_CHUNK, dh).transpose(0, 1, 3, 2, 4).reshape(bsz, seq, heads * dh)


def ssd_chunked_scan(xs, dt, a, bm, cm):
    bsz, seq, heads, hp = xs.shape
    groups, n_st = bm.shape[2], bm.shape[3]
    rep = heads // groups
    L = SSD_CHUNK
    xs_p, dt_p, bm_p, cm_p = [pad_seq(t, L) for t in (xs, dt, bm, cm)]
    n_c = xs_p.shape[1] // L
    x_dt = (xs_p * dt_p[..., None].astype(xs.dtype)).reshape(bsz, n_c, L, groups, rep, hp)
    bm_c = bm_p.reshape(bsz, n_c, L, groups, n_st)
    cm_c = cm_p.reshape(bsz, n_c, L, groups, n_st)
    a_dt = (dt_p * a).reshape(bsz, n_c, L, groups, rep).transpose(0, 3, 4, 1, 2)
    a_cs = jnp.cumsum(a_dt, axis=-1)
    causal = jnp.tril(jnp.ones((L, L), dtype=bool))
    seg = jnp.where(causal, a_cs[..., :, None] - a_cs[..., None, :], -jnp.inf)
    decay_in = jnp.exp(seg).astype(xs.dtype)
    cb = jnp.einsum('bclgn,bcsgn->bgcls', cm_c, bm_c)
    y_diag = jnp.einsum('bgrcls,bcsgrp->bclgrp', cb[:, :, None] * decay_in, x_dt)
    decay_to_end = jnp.exp(a_cs[..., -1:] - a_cs).astype(xs.dtype).transpose(0, 3, 4, 1, 2)
    chunk_states = jnp.einsum('bcsgn,bcsgrp->bcgrpn', bm_c, x_dt * decay_to_end[..., None])
    chunk_decay = jnp.exp(a_cs[..., -1])

    def step(h, inp):
        st, dec = inp
        return h * dec[..., None, None] + st, h

    h0 = jnp.zeros((bsz, groups, rep, hp, n_st), jnp.float32)
    _, h_in = lax.scan(step, h0, (jnp.moveaxis(chunk_states.astype(jnp.float32), 1, 0),
                                  jnp.moveaxis(chunk_decay, -1, 0)))
    decay_from_start = jnp.exp(a_cs).astype(xs.dtype).transpose(0, 3, 4, 1, 2)
    y_off = jnp.einsum('bclgn,cbgrpn->bclgrp', cm_c, h_in.astype(xs.dtype)) * decay_from_start[..., None]
    return (y_diag + y_off).reshape(bsz, n_c * L, heads, hp)[:, :seq]


def gated_rmsnorm(y, z, w):
    g = (y * jax.nn.silu(z)).astype(jnp.float32)
    sh = g.shape
    g = g.reshape(sh[0], sh[1], SSD_GROUPS, -1)
    g = g * lax.rsqrt(jnp.mean(jnp.square(g), axis=-1, keepdims=True) + RMS_EPS)
    return (g.reshape(sh) * w.astype(jnp.float32)).astype(y.dtype)


def moba_ssd_mixer(x, w_in, conv_w, conv_b, dt_bias, a_log, d_skip, norm_w, w_out):
    bsz, seq, _ = x.shape
    h = x @ w_in
    cuts = [ATTN_WIDTH, 2 * ATTN_WIDTH, 3 * ATTN_WIDTH, 3 * ATTN_WIDTH + SSD_INNER,
            3 * ATTN_WIDTH + SSD_INNER + SSD_CONV_CH]
    q, k, v, z, xbc, dt = jnp.split(h, cuts, axis=-1)
    hs = (bsz, seq, ATTN_HEADS, ATTN_HEAD_DIM)
    attn = moba_attention(q.reshape(hs), k.reshape(hs), v.reshape(hs))
    xbc = jax.nn.silu(causal_depthwise_conv(xbc, conv_w, conv_b))
    xs, bm, cm = jnp.split(xbc, [SSD_INNER, SSD_INNER + SSD_GROUPS * SSD_STATE], axis=-1)
    xs = xs.reshape(bsz, seq, SSD_HEADS, SSD_HEAD_DIM)
    bm = bm.reshape(bsz, seq, SSD_GROUPS, SSD_STATE)
    cm = cm.reshape(bsz, seq, SSD_GROUPS, SSD_STATE)
    dt = jax.nn.softplus((dt + dt_bias).astype(jnp.float32))
    a = -jnp.exp(a_log.astype(jnp.float32))
    y = ssd_chunked_scan(xs, dt, a, bm, cm) + xs * d_skip[:, None]
    y = gated_rmsnorm(y.reshape(bsz, seq, SSD_INNER), z, norm_w)
    return jnp.concatenate([attn, y], axis=-1) @ w_out


def conformer_conv_module(x, w_pw1, b_pw1, dw_w, dw_b, ln_g, ln_b, w_pw2, b_pw2):
    h = x @ w_pw1 + b_pw1
    val, gate = jnp.split(h, 2, axis=-1)
    h = val * jax.nn.sigmoid(gate)
    h = causal_depthwise_conv(h, dw_w, dw_b)
    h = jax.nn.silu(layer_norm(h, ln_g, ln_b))
    return h @ w_pw2 + b_pw2


def swiglu_ffn(x, w_gate, w_up, w_down):
    return (jax.nn.silu(x @ w_gate) * (x @ w_up)) @ w_down


def setup_inputs(seed: int = 0) -> dict:
    key = jax.random.key(seed)
    ks = jax.random.split(key, 32)
    f32 = jnp.float32

    def nrm(k, shape, scale):
        return jax.random.normal(k, shape, f32) * scale

    dt_init = jnp.exp(jax.random.uniform(ks[5], (N_EVEN, SSD_HEADS), f32)
                      * (np.log(0.1) - np.log(0.001)) + np.log(0.001))
    inp = {
        'x': nrm(ks[0], (BATCH, SEQ, D_MODEL), 1.0),
        'mix_w_in': nrm(ks[1], (N_EVEN, D_MODEL, IN_COLS), D_MODEL ** -0.5),
        'ssd_conv_w': nrm(ks[2], (N_EVEN, SSD_CONV, SSD_CONV_CH), SSD_CONV ** -0.5),
        'ssd_conv_b': nrm(ks[3], (N_EVEN, SSD_CONV_CH), 0.02),
        'ssd_dt_bias': dt_init + jnp.log(-jnp.expm1(-dt_init)),
        'ssd_a_log': jnp.log(jax.random.uniform(ks[6], (N_EVEN, SSD_HEADS), f32, 1.0, 16.0)),
        'ssd_d': 1.0 + nrm(ks[7], (N_EVEN, SSD_HEADS), 0.02),
        'ssd_norm_w': 1.0 + nrm(ks[8], (N_EVEN, SSD_INNER), 0.02),
        'mix_w_out': nrm(ks[9], (N_EVEN, MIX_WIDTH, D_MODEL), MIX_WIDTH ** -0.5 * DN_BETA),
        'conv_w_pw1': nrm(ks[10], (N_ODD, D_MODEL, 2 * D_MODEL), D_MODEL ** -0.5),
        'conv_b_pw1': nrm(ks[11], (N_ODD, 2 * D_MODEL), 0.02),
        'conv_dw_w': nrm(ks[12], (N_ODD, CONF_KERNEL, D_MODEL), CONF_KERNEL ** -0.5),
        'conv_dw_b': nrm(ks[13], (N_ODD, D_MODEL), 0.02),
        'conv_ln_g': 1.0 + nrm(ks[14], (N_ODD, D_MODEL), 0.02),
        'conv_ln_b': nrm(ks[15], (N_ODD, D_MODEL), 0.02),
        'conv_w_pw2': nrm(ks[16], (N_ODD, D_MODEL, D_MODEL), D_MODEL ** -0.5 * DN_BETA),
        'conv_b_pw2': nrm(ks[17], (N_ODD, D_MODEL), 0.02),
        'ffn_w_gate': nrm(ks[18], (DEPTH, D_MODEL, FFN_HIDDEN), D_MODEL ** -0.5),
        'ffn_w_up': nrm(ks[19], (DEPTH, D_MODEL, FFN_HIDDEN), D_MODEL ** -0.5),
        'ffn_w_down': nrm(ks[20], (DEPTH, FFN_HIDDEN, D_MODEL), FFN_HIDDEN ** -0.5 * DN_BETA),
        'ln_mix_g': 1.0 + nrm(ks[21], (DEPTH, D_MODEL), 0.02),
        'ln_mix_b': nrm(ks[22], (DEPTH, D_MODEL), 0.02),
        'ln_ffn_g': 1.0 + nrm(ks[23], (DEPTH, D_MODEL), 0.02),
        'ln_ffn_b': nrm(ks[24], (DEPTH, D_MODEL), 0.02),
    }
    return inp


def reference(x, mix_w_in, ssd_conv_w, ssd_conv_b, ssd_dt_bias, ssd_a_log, ssd_d, ssd_norm_w,
              mix_w_out, conv_w_pw1, conv_b_pw1, conv_dw_w, conv_dw_b, conv_ln_g, conv_ln_b,
              conv_w_pw2, conv_b_pw2, ffn_w_gate, ffn_w_up, ffn_w_down,
              ln_mix_g, ln_mix_b, ln_ffn_g, ln_ffn_b):
    for layer in range(DEPTH):
        j = layer // 2
        if layer % 2 == 0:
            mixed = moba_ssd_mixer(x, mix_w_in[j], ssd_conv_w[j], ssd_conv_b[j], ssd_dt_bias[j],
                                   ssd_a_log[j], ssd_d[j], ssd_norm_w[j], mix_w_out[j])
        else:
            mixed = conformer_conv_module(x, conv_w_pw1[j], conv_b_pw1[j], conv_dw_w[j], conv_dw_b[j],
                                          conv_ln_g[j], conv_ln_b[j], conv_w_pw2[j], conv_b_pw2[j])
        x = layer_norm(DN_ALPHA * x + mixed, ln_mix_g[layer], ln_mix_b[layer])
        x = layer_norm(DN_ALPHA * x + swiglu_ffn(x, ffn_w_gate[layer], ffn_w_up[layer], ffn_w_down[layer]),
                       ln_ffn_g[layer], ln_ffn_b[layer])
    return x
```

```python
import functools

import jax
import jax.numpy as jnp
from jax import lax
from jax.experimental import pallas as pl
from jax.experimental.pallas import tpu as pltpu

F32 = jnp.float32
BF16 = jnp.bfloat16

D_MODEL = 1024
DEPTH = 2
ATTN_WIDTH = 512
HEAD_DIM = 64
MOBA_BLOCK = 256
MOBA_TOPK = 3
SSD_INNER = 512
SSD_HEADS = 8
SSD_GROUPS = 2
SSD_STATE = 128
SSD_CONV = 4
SSD_CHUNK = 256
SSD_CONV_CH = SSD_INNER + 2 * SSD_GROUPS * SSD_STATE
CONF_KERNEL = 31
DN_ALPHA = (2 * DEPTH) ** 0.25
LN_EPS = 1e-5
RMS_EPS = 1e-5

LANES = 128
SUBLANES = 8

COL_Q = 0
COL_K = ATTN_WIDTH
COL_V = 2 * ATTN_WIDTH
COL_Z = 3 * ATTN_WIDTH
COL_XBC = COL_Z + SSD_INNER
COL_DT = COL_XBC + SSD_CONV_CH
IN_COLS_PAD = COL_DT + LANES

MASK_NEG = -1e30

ROW_TILE = 512
INPROJ_COL_TILE = 640
FFN_HID_TILE = 1408
GLU_COL_TILE = 512
CONV_SEQ_TILE = 512
CONV_HALO = 32
CONV_ROW_CHUNK = 64
CONV_COL_CHUNK = 256
SSD_HALO = 8
VMEM_LIMIT = 56 * 1024 * 1024


def _params(*sem):
    return pltpu.CompilerParams(dimension_semantics=sem, vmem_limit_bytes=VMEM_LIMIT)


def _dot(a, b):
    return jnp.dot(a, b, preferred_element_type=F32)


def _dot_nt(a, b):
    return lax.dot_general(a, b, (((1,), (1,)), ((), ())), preferred_element_type=F32)


def _split3(x):
    hi = x.astype(BF16)
    r1 = x - hi.astype(F32)
    mid = r1.astype(BF16)
    lo = (r1 - mid.astype(F32)).astype(BF16)
    return hi, mid, lo


def _dot_exact_rhs(x, rhs_bf16):
    hi, mid, lo = _split3(x)
    return _dot(hi, rhs_bf16) + _dot(mid, rhs_bf16) + _dot(lo, rhs_bf16)


def _silu(x):
    return x * jax.nn.sigmoid(x)


def _layer_norm(r, g, b):
    mu = jnp.mean(r, axis=-1, keepdims=True)
    d = r - mu
    var = jnp.mean(d * d, axis=-1, keepdims=True)
    return d * lax.rsqrt(var + LN_EPS) * g + b


def _inproj_kernel(x_ref, w_ref, o_ref):
    o_ref[...] = _dot(x_ref[...].astype(BF16), w_ref[...])


def _inproj(x2d, w_pad):
    t = x2d.shape[0]
    return pl.pallas_call(
        _inproj_kernel,
        out_shape=jax.ShapeDtypeStruct((t, IN_COLS_PAD), F32),
        grid=(t // ROW_TILE, IN_COLS_PAD // INPROJ_COL_TILE),
        in_specs=[
            pl.BlockSpec((ROW_TILE, D_MODEL), lambda i, j: (i, 0)),
            pl.BlockSpec((D_MODEL, INPROJ_COL_TILE), lambda i, j: (0, j)),
        ],
        out_specs=pl.BlockSpec((ROW_TILE, INPROJ_COL_TILE), lambda i, j: (i, j)),
        compiler_params=_params("parallel", "arbitrary"),
        name="inproj",
    )(x2d, w_pad)


def _moba_kernel(slopes_ref, q_ref, k_ref, v_ref, o_ref, kb_ref, vt_ref, km_ref, rb_ref):
    hp = pl.program_id(1)
    i = pl.program_id(2)
    nb = kb_ref.shape[0]
    blk = MOBA_BLOCK
    scale = HEAD_DIM ** -0.5

    @pl.when(i == 0)
    def _():
        for n in range(nb):
            kblk = k_ref[0, n * blk:(n + 1) * blk, :]
            kb_ref[n] = kblk.astype(BF16)
            km_ref[n:n + 1, :] = jnp.sum(kblk, axis=0, keepdims=True) * (1.0 / blk)
            vt_ref[n] = v_ref[0, n * blk:(n + 1) * blk, :].T.astype(BF16)

    q = q_ref[0]
    lane = lax.broadcasted_iota(jnp.int32, (blk, LANES), 1)
    blk_id = lax.broadcasted_iota(jnp.int32, (nb, blk), 0)
    key_in = lax.broadcasted_iota(jnp.int32, (blk, blk), 0)
    qry_in = lax.broadcasted_iota(jnp.int32, (blk, blk), 1)
    dist0 = (qry_in - key_in).astype(F32)
    causal = qry_in >= key_in
    km = km_ref[...]

    outs = []
    for hd in range(2):
        head_lanes = (lane >= hd * HEAD_DIM) & (lane < (hd + 1) * HEAD_DIM)
        slope = slopes_ref[2 * hp + hd]
        q_h = jnp.where(head_lanes, q, 0.0)
        qh3 = _split3(q_h)
        km3 = _split3(km)
        gate = (_dot_nt(km3[0], qh3[0]) + _dot_nt(km3[0], qh3[1]) + _dot_nt(km3[1], qh3[0])
                + _dot_nt(km3[1], qh3[1]) + _dot_nt(km3[0], qh3[2]) + _dot_nt(km3[2], qh3[0]))
        rank = jnp.zeros(gate.shape, jnp.int32)
        for m in range(nb):
            gm = gate[m:m + 1, :]
            beats = (gm > gate) | ((gm == gate) & (blk_id > m))
            counted = beats & (blk_id != m)
            rank = rank + jnp.where(counted, 1, 0) * jnp.where(m < i, 1, 0)
        selected = (blk_id < i) & (rank < MOBA_TOPK)
        blk_dist = ((i - blk_id) * blk).astype(F32)
        rb_ref[hd] = jnp.where(selected, -slope * blk_dist, MASK_NEG)

        qs = (q_h * scale).astype(BF16)
        alibi = -slope * dist0
        vt_rows = slice(hd * HEAD_DIM, (hd + 1) * HEAD_DIM)

        s = _dot_nt(kb_ref[i], qs) + jnp.where(causal, alibi, MASK_NEG)
        m0 = jnp.max(s, axis=0, keepdims=True)
        p = jnp.exp(s - m0)
        l0 = jnp.sum(p, axis=0, keepdims=True)
        acc0 = _dot(vt_ref[i, vt_rows, :], p.astype(BF16))

        def body(j, carry):
            m_run, l_run, acc = carry
            rb = rb_ref[hd, pl.ds(j, 1), :]
            sj = _dot_nt(kb_ref[j], qs) + alibi + rb
            m_new = jnp.maximum(m_run, jnp.max(sj, axis=0, keepdims=True))
            a = jnp.exp(m_run - m_new)
            pj = jnp.exp(sj - m_new)
            l_new = a * l_run + jnp.sum(pj, axis=0, keepdims=True)
            acc_new = a * acc + _dot(vt_ref[j, vt_rows, :], pj.astype(BF16))
            return m_new, l_new, acc_new

        _, l_fin, acc_fin = lax.fori_loop(0, i, body, (m0, l0, acc0))
        outs.append(acc_fin / l_fin)

    o_ref[0] = jnp.concatenate(outs, axis=0).T


def _moba(h3d, slopes):
    bsz, seq, _ = h3d.shape
    nb = seq // MOBA_BLOCK
    n_pairs = ATTN_WIDTH // LANES
    return pl.pallas_call(
        _moba_kernel,
        out_shape=jax.ShapeDtypeStruct((bsz, seq, ATTN_WIDTH), F32),
        grid=(bsz, n_pairs, nb),
        in_specs=[
            pl.BlockSpec(memory_space=pltpu.SMEM),
            pl.BlockSpec((1, MOBA_BLOCK, LANES), lambda b, p, i: (b, i, COL_Q // LANES + p)),
            pl.BlockSpec((1, seq, LANES), lambda b, p, i: (b, 0, COL_K // LANES + p)),
            pl.BlockSpec((1, seq, LANES), lambda b, p, i: (b, 0, COL_V // LANES + p)),
        ],
        out_specs=pl.BlockSpec((1, MOBA_BLOCK, LANES), lambda b, p, i: (b, i, p)),
        scratch_shapes=[
            pltpu.VMEM((nb, MOBA_BLOCK, LANES), BF16),
            pltpu.VMEM((nb, LANES, MOBA_BLOCK), BF16),
            pltpu.VMEM((nb, LANES), F32),
            pltpu.VMEM((2, nb, MOBA_BLOCK), F32),
        ],
        compiler_params=_params("parallel", "parallel", "arbitrary"),
        name="moba_attention",
    )(slopes, h3d, h3d, h3d)


def _ssd_kernel(xbc_ref, dt_ref, z_ref, cw_ref, cb_ref, dtb_ref, alog_ref, dskip_ref, nw_ref,
                y_ref, xpad_ref, hst_ref):
    c = pl.program_id(1)
    L = SSD_CHUNK
    gw = SSD_INNER // SSD_GROUPS

    @pl.when(c == 0)
    def _():
        xpad_ref[0:SSD_HALO, :] = jnp.zeros((SSD_HALO, SSD_CONV_CH), F32)
        hst_ref[...] = jnp.zeros(hst_ref.shape, F32)

    xpad_ref[SSD_HALO:SSD_HALO + L, :] = xbc_ref[0]
    conv = jnp.broadcast_to(cb_ref[...], (L, SSD_CONV_CH))
    for k in range(SSD_CONV):
        off = SSD_HALO - (SSD_CONV - 1) + k
        conv = conv + cw_ref[k:k + 1, :] * xpad_ref[off:off + L, :]
    xpad_ref[0:SSD_HALO, :] = xpad_ref[L:L + SSD_HALO, :]
    xbc = _silu(conv)
    xs = xbc[:, :SSD_INNER]

    dt_in = dt_ref[0] + dtb_ref[...]
    dt = jnp.maximum(dt_in, 0.0) + jnp.log1p(jnp.exp(-jnp.abs(dt_in)))
    a = -jnp.exp(alog_ref[...])
    a_dt = dt * a
    row = lax.broadcasted_iota(jnp.int32, (L, L), 0)
    col = lax.broadcasted_iota(jnp.int32, (L, L), 1)
    causal = row >= col
    tril = jnp.where(causal, 1.0, 0.0).astype(BF16)
    a_cs = _dot_exact_rhs_lhs(tril, a_dt)
    a_cs_t = a_cs.T
    a_last = a_cs[L - 1:L, :]
    decay_to_end = jnp.exp(a_last - a_cs)
    decay_from_start = jnp.exp(a_cs)

    eh = lax.broadcasted_iota(jnp.int32, (LANES, SSD_INNER), 0)
    ec = lax.broadcasted_iota(jnp.int32, (LANES, SSD_INNER), 1)
    expand = jnp.where((ec >= eh * HEAD_DIM) & (ec < (eh + 1) * HEAD_DIM), 1.0, 0.0).astype(BF16)
    dt_full = _dot_exact_rhs(dt, expand)
    dte_full = _dot_exact_rhs(decay_to_end, expand)
    dfs_full = _dot_exact_rhs(decay_from_start, expand)

    x_dt = xs * dt_full
    x_dte = (x_dt * dte_full).astype(BF16)
    x_dt_b = x_dt.astype(BF16)
    first_head = lax.broadcasted_iota(jnp.int32, (L, LANES), 1) < HEAD_DIM
    heads_per_group = SSD_HEADS // SSD_GROUPS

    for g in range(SSD_GROUPS):
        bm = xbc[:, SSD_INNER + g * SSD_STATE:SSD_INNER + (g + 1) * SSD_STATE]
        cm = xbc[:, SSD_INNER + (SSD_GROUPS + g) * SSD_STATE:SSD_INNER + (SSD_GROUPS + g + 1) * SSD_STATE]
        bm_b = bm.astype(BF16)
        cm_b = cm.astype(BF16)
        cbm = _dot_nt(cm_b, bm_b)
        gs = slice(g * gw, (g + 1) * gw)

        h_prev = hst_ref[g]
        y_g = _dot(cm_b, h_prev.astype(BF16)) * dfs_full[:, gs]
        st = _dot(bm.T.astype(BF16), x_dte[:, gs])
        hst_ref[g] = h_prev * dfs_full[L - 1:L, gs] + st

        pieces = []
        for pi in range(heads_per_group // 2):
            ls = slice(g * gw + pi * LANES, g * gw + (pi + 1) * LANES)
            xp = x_dt_b[:, ls]
            yd = []
            for hd in range(2):
                h = g * heads_per_group + 2 * pi + hd
                seg = a_cs[:, h:h + 1] - a_cs_t[h:h + 1, :]
                dec = jnp.exp(jnp.where(causal, seg, MASK_NEG))
                yd.append(_dot((cbm * dec).astype(BF16), xp))
            pieces.append(jnp.where(first_head, yd[0], yd[1]))
        y_g = y_g + jnp.concatenate(pieces, axis=1) + xs[:, gs] * dskip_ref[:, gs]

        gz = y_g * _silu(z_ref[0, :, gs])
        ms = jnp.mean(gz * gz, axis=-1, keepdims=True)
        y_ref[0, :, gs] = gz * lax.rsqrt(ms + RMS_EPS) * nw_ref[:, gs]


def _dot_exact_rhs_lhs(lhs_bf16, x):
    hi, mid, lo = _split3(x)
    return _dot(lhs_bf16, hi) + _dot(lhs_bf16, mid) + _dot(lhs_bf16, lo)


def _ssd(h3d, conv_w, conv_b, dt_bias, a_log, d_full, norm_w):
    bsz, seq, _ = h3d.shape
    n_c = seq // SSD_CHUNK
    const = lambda b, c: (0, 0)
    return pl.pallas_call(
        _ssd_kernel,
        out_shape=jax.ShapeDtypeStruct((bsz, seq, SSD_INNER), F32),
        grid=(bsz, n_c),
        in_specs=[
            pl.BlockSpec((1, SSD_CHUNK, SSD_CONV_CH), lambda b, c: (b, c, COL_XBC // SSD_CONV_CH)),
            pl.BlockSpec((1, SSD_CHUNK, LANES), lambda b, c: (b, c, COL_DT // LANES)),
            pl.BlockSpec((1, SSD_CHUNK, SSD_INNER), lambda b, c: (b, c, COL_Z // SSD_INNER)),
            pl.BlockSpec((SSD_CONV, SSD_CONV_CH), const),
            pl.BlockSpec((1, SSD_CONV_CH), const),
            pl.BlockSpec((1, LANES), const),
            pl.BlockSpec((1, LANES), const),
            pl.BlockSpec((1, SSD_INNER), const),
            pl.BlockSpec((1, SSD_INNER), const),
        ],
        out_specs=pl.BlockSpec((1, SSD_CHUNK, SSD_INNER), lambda b, c: (b, c, 0)),
        scratch_shapes=[
            pltpu.VMEM((SSD_HALO + SSD_CHUNK, SSD_CONV_CH), F32),
            pltpu.VMEM((SSD_GROUPS, SSD_STATE, SSD_INNER // SSD_GROUPS), F32),
        ],
        compiler_params=_params("parallel", "arbitrary"),
        name="ssd_scan",
    )(h3d, h3d, h3d, conv_w, conv_b, dt_bias, a_log, d_full, norm_w)


def _outproj_ln_kernel(attn_ref, y_ref, x_ref, w_ref, g_ref, b_ref, o_ref):
    mixed = (_dot(attn_ref[...].astype(BF16), w_ref[0:ATTN_WIDTH, :])
             + _dot(y_ref[...].astype(BF16), w_ref[ATTN_WIDTH:, :]))
    o_ref[...] = _layer_norm(DN_ALPHA * x_ref[...] + mixed, g_ref[...], b_ref[...])


def _outproj_ln(attn2d, y2d, x2d, w_out, g, b):
    t = x2d.shape[0]
    const = lambda i: (0, 0)
    return pl.pallas_call(
        _outproj_ln_kernel,
        out_shape=jax.ShapeDtypeStruct((t, D_MODEL), F32),
        grid=(t // ROW_TILE,),
        in_specs=[
            pl.BlockSpec((ROW_TILE, ATTN_WIDTH), lambda i: (i, 0)),
            pl.BlockSpec((ROW_TILE, SSD_INNER), lambda i: (i, 0)),
            pl.BlockSpec((ROW_TILE, D_MODEL), lambda i: (i, 0)),
            pl.BlockSpec((ATTN_WIDTH + SSD_INNER, D_MODEL), const),
            pl.BlockSpec((1, D_MODEL), const),
            pl.BlockSpec((1, D_MODEL), const),
        ],
        out_specs=pl.BlockSpec((ROW_TILE, D_MODEL), lambda i: (i, 0)),
        compiler_params=_params("parallel"),
        name="outproj_ln",
    )(attn2d, y2d, x2d, w_out, g, b)


def _ffn_ln_kernel(x_ref, wg_ref, wu_ref, wd_ref, g_ref, b_ref, o_ref, acc_ref):
    k = pl.program_id(1)
    xb = x_ref[...].astype(BF16)
    gate = _dot(xb, wg_ref[...])
    up = _dot(xb, wu_ref[...])
    part = _dot((_silu(gate) * up).astype(BF16), wd_ref[...])

    @pl.when(k == 0)
    def _():
        acc_ref[...] = part

    @pl.when(k > 0)
    def _():
        acc_ref[...] += part

    @pl.when(k == pl.num_programs(1) - 1)
    def _():
        o_ref[...] = _layer_norm(DN_ALPHA * x_ref[...] + acc_ref[...], g_ref[...], b_ref[...])


def _ffn_ln(x2d, w_gate, w_up, w_down, g, b):
    t = x2d.shape[0]
    hid = w_gate.shape[1]
    const = lambda i, k: (0, 0)
    return pl.pallas_call(
        _ffn_ln_kernel,
        out_shape=jax.ShapeDtypeStruct((t, D_MODEL), F32),
        grid=(t // ROW_TILE, hid // FFN_HID_TILE),
        in_specs=[
            pl.BlockSpec((ROW_TILE, D_MODEL), lambda i, k: (i, 0)),
            pl.BlockSpec((D_MODEL, FFN_HID_TILE), lambda i, k: (0, k)),
            pl.BlockSpec((D_MODEL, FFN_HID_TILE), lambda i, k: (0, k)),
            pl.BlockSpec((FFN_HID_TILE, D_MODEL), lambda i, k: (k, 0)),
            pl.BlockSpec((1, D_MODEL), const),
            pl.BlockSpec((1, D_MODEL), const),
        ],
        out_specs=pl.BlockSpec((ROW_TILE, D_MODEL), lambda i, k: (i, 0)),
        scratch_shapes=[pltpu.VMEM((ROW_TILE, D_MODEL), F32)],
        compiler_params=_params("parallel", "arbitrary"),
        name="ffn_ln",
    )(x2d, w_gate, w_up, w_down, g, b)


def _glu_kernel(x_ref, wv_ref, wg_ref, bv_ref, bg_ref, o_ref):
    xb = x_ref[...].astype(BF16)
    val = _dot(xb, wv_ref[...]) + bv_ref[...]
    gate = _dot(xb, wg_ref[...]) + bg_ref[...]
    o_ref[...] = val * jax.nn.sigmoid(gate)


def _pw1_glu(x2d, w_pw1, b_pw1):
    t = x2d.shape[0]
    n_col = D_MODEL // GLU_COL_TILE
    return pl.pallas_call(
        _glu_kernel,
        out_shape=jax.ShapeDtypeStruct((t, D_MODEL), F32),
        grid=(t // ROW_TILE, n_col),
        in_specs=[
            pl.BlockSpec((ROW_TILE, D_MODEL), lambda i, j: (i, 0)),
            pl.BlockSpec((D_MODEL, GLU_COL_TILE), lambda i, j: (0, j)),
            pl.BlockSpec((D_MODEL, GLU_COL_TILE), lambda i, j: (0, n_col + j)),
            pl.BlockSpec((1, GLU_COL_TILE), lambda i, j: (0, j)),
            pl.BlockSpec((1, GLU_COL_TILE), lambda i, j: (0, n_col + j)),
        ],
        out_specs=pl.BlockSpec((ROW_TILE, GLU_COL_TILE), lambda i, j: (i, j)),
        compiler_params=_params("parallel", "arbitrary"),
        name="pw1_glu",
    )(x2d, w_pw1, w_pw1, b_pw1, b_pw1)


def _conv_pw2_ln_kernel(h_ref, x_ref, dww_ref, dwb_ref, cg_ref, cb_ref, w2_ref, b2_ref,
                        g_ref, b_ref, o_ref, hpad_ref, cv_ref):
    t = pl.program_id(1)
    ts = CONV_SEQ_TILE

    @pl.when(t == 0)
    def _():
        hpad_ref[0:CONV_HALO, :] = jnp.zeros((CONV_HALO, D_MODEL), F32)

    hpad_ref[CONV_HALO:CONV_HALO + ts, :] = h_ref[0]
    base = CONV_HALO - (CONF_KERNEL - 1)

    def row_chunk(rc, carry):
        r0 = pl.multiple_of(rc * CONV_ROW_CHUNK, CONV_ROW_CHUNK)
        for cc in range(D_MODEL // CONV_COL_CHUNK):
            cs = slice(cc * CONV_COL_CHUNK, (cc + 1) * CONV_COL_CHUNK)
            acc = jnp.broadcast_to(dwb_ref[:, cs], (CONV_ROW_CHUNK, CONV_COL_CHUNK))
            window = hpad_ref.at[pl.ds(r0, CONV_ROW_CHUNK + CONV_HALO), cs]
            for k in range(CONF_KERNEL):
                acc = acc + dww_ref[k:k + 1, cs] * window[base + k:base + k + CONV_ROW_CHUNK, :]
            cv_ref[pl.ds(r0, CONV_ROW_CHUNK), cs] = acc
        return carry

    lax.fori_loop(0, ts // CONV_ROW_CHUNK, row_chunk, 0)
    hpad_ref[0:CONV_HALO, :] = hpad_ref[ts:ts + CONV_HALO, :]

    hn = _silu(_layer_norm(cv_ref[...], cg_ref[...], cb_ref[...]))
    mixed = _dot(hn.astype(BF16), w2_ref[...]) + b2_ref[...]
    o_ref[0] = _layer_norm(DN_ALPHA * x_ref[0] + mixed, g_ref[...], b_ref[...])


def _conv_pw2_ln(h3d, x3d, dw_w, dw_b, cg, cb, w2, b2, g, b):
    bsz, seq, _ = x3d.shape
    const = lambda bb, tt: (0, 0)
    vec = pl.BlockSpec((1, D_MODEL), const)
    return pl.pallas_call(
        _conv_pw2_ln_kernel,
        out_shape=jax.ShapeDtypeStruct((bsz, seq, D_MODEL), F32),
        grid=(bsz, seq // CONV_SEQ_TILE),
        in_specs=[
            pl.BlockSpec((1, CONV_SEQ_TILE, D_MODEL), lambda bb, tt: (bb, tt, 0)),
            pl.BlockSpec((1, CONV_SEQ_TILE, D_MODEL), lambda bb, tt: (bb, tt, 0)),
            pl.BlockSpec((CONV_HALO, D_MODEL), const),
            vec, vec, vec,
            pl.BlockSpec((D_MODEL, D_MODEL), const),
            vec, vec, vec,
        ],
        out_specs=pl.BlockSpec((1, CONV_SEQ_TILE, D_MODEL), lambda bb, tt: (bb, tt, 0)),
        scratch_shapes=[
            pltpu.VMEM((CONV_HALO + CONV_SEQ_TILE, D_MODEL), F32),
            pltpu.VMEM((CONV_SEQ_TILE, D_MODEL), F32),
        ],
        compiler_params=_params("parallel", "arbitrary"),
        name="conv_pw2_ln",
    )(h3d, x3d, dw_w, dw_b, cg, cb, w2, b2, g, b)


def _row(v):
    return v.reshape(1, -1).astype(F32)


def _pad_lanes(v):
    return jnp.pad(v.reshape(1, -1).astype(F32), ((0, 0), (0, LANES - v.shape[-1])))


def kernel(x, mix_w_in, ssd_conv_w, ssd_conv_b, ssd_dt_bias, ssd_a_log, ssd_d, ssd_norm_w, mix_w_out, conv_w_pw1, conv_b_pw1, conv_dw_w, conv_dw_b, conv_ln_g, conv_ln_b, conv_w_pw2, conv_b_pw2, ffn_w_gate, ffn_w_up, ffn_w_down, ln_mix_g, ln_mix_b, ln_ffn_g, ln_ffn_b):
    bsz, seq, d = x.shape
    t = bsz * seq
    n_heads = ATTN_WIDTH // HEAD_DIM
    slopes = 2.0 ** (-8.0 * jnp.arange(1, n_heads + 1, dtype=F32) / n_heads)

    w_in = jnp.pad(mix_w_in[0], ((0, 0), (0, IN_COLS_PAD - mix_w_in.shape[-1]))).astype(BF16)
    h = _inproj(x.reshape(t, d), w_in).reshape(bsz, seq, IN_COLS_PAD)
    attn = _moba(h, slopes)
    y = _ssd(h, ssd_conv_w[0], _row(ssd_conv_b[0]), _pad_lanes(ssd_dt_bias[0]), _pad_lanes(ssd_a_log[0]),
             _row(jnp.repeat(ssd_d[0], HEAD_DIM)), _row(ssd_norm_w[0]))
    x1 = _outproj_ln(attn.reshape(t, ATTN_WIDTH), y.reshape(t, SSD_INNER), x.reshape(t, d),
                     mix_w_out[0].astype(BF16), _row(ln_mix_g[0]), _row(ln_mix_b[0]))
    x2 = _ffn_ln(x1, ffn_w_gate[0].astype(BF16), ffn_w_up[0].astype(BF16), ffn_w_down[0].astype(BF16),
                 _row(ln_ffn_g[0]), _row(ln_ffn_b[0]))

    hc = _pw1_glu(x2, conv_w_pw1[0].astype(BF16), _row(conv_b_pw1[0]))
    dw_w = jnp.pad(conv_dw_w[0], ((0, CONV_HALO - CONF_KERNEL), (0, 0)))
    x3 = _conv_pw2_ln(hc.reshape(bsz, seq, d), x2.reshape(bsz, seq, d), dw_w, _row(conv_dw_b[0]),
                      _row(conv_ln_g[0]), _row(conv_ln_b[0]), conv_w_pw2[0].astype(BF16),
                      _row(conv_b_pw2[0]), _row(ln_mix_g[1]), _row(ln_mix_b[1]))
    x4 = _ffn_ln(x3.reshape(t, d), ffn_w_gate[1].astype(BF16), ffn_w_up[1].astype(BF16),
                 ffn_w_down[1].astype(BF16), _row(ln_ffn_g[1]), _row(ln_ffn_b[1]))
    return x4.reshape(bsz, seq, d)
```

```python
import functools

import jax
import jax.numpy as jnp
from jax import lax
from jax.experimental import pallas as pl
from jax.experimental.pallas import tpu as pltpu

F32 = jnp.float32
BF16 = jnp.bfloat16

D_MODEL = 1024
DEPTH = 2
ATTN_WIDTH = 512
HEAD_DIM = 64
MOBA_BLOCK = 256
MOBA_TOPK = 3
SSD_INNER = 512
SSD_HEADS = 8
SSD_GROUPS = 2
SSD_STATE = 128
SSD_CONV = 4
SSD_CHUNK = 256
SSD_CONV_CH = SSD_INNER + 2 * SSD_GROUPS * SSD_STATE
CONF_KERNEL = 31
DN_ALPHA = (2 * DEPTH) ** 0.25
LN_EPS = 1e-5
RMS_EPS = 1e-5

LANES = 128
SUBLANES = 8

COL_Q = 0
COL_K = ATTN_WIDTH
COL_V = 2 * ATTN_WIDTH
COL_Z = 3 * ATTN_WIDTH
COL_XBC = COL_Z + SSD_INNER
COL_DT = COL_XBC + SSD_CONV_CH
IN_COLS_PAD = COL_DT + LANES

MASK_NEG = -1e30

ROW_TILE = 512
INPROJ_COL_CHUNK = 512
GLU_COL_TILE = 512
CONV_SEQ_TILE = 512
CONV_HALO = 32
CONV_ROW_CHUNK = 64
CONV_COL_CHUNK = 256
SSD_HALO = 8
VMEM_LIMIT = 56 * 1024 * 1024


def _params(*sem):
    return pltpu.CompilerParams(dimension_semantics=sem, vmem_limit_bytes=VMEM_LIMIT)


def _dot(a, b):
    return jnp.dot(a, b, preferred_element_type=F32)


def _dot_nt(a, b):
    return lax.dot_general(a, b, (((1,), (1,)), ((), ())), preferred_element_type=F32)


def _split3(x):
    hi = x.astype(BF16)
    r1 = x - hi.astype(F32)
    mid = r1.astype(BF16)
    lo = (r1 - mid.astype(F32)).astype(BF16)
    return hi, mid, lo


def _dot_exact_rhs(x, rhs_bf16):
    hi, mid, lo = _split3(x)
    return _dot(hi, rhs_bf16) + _dot(mid, rhs_bf16) + _dot(lo, rhs_bf16)


def _silu(x):
    return x * jax.nn.sigmoid(x)


def _layer_norm(r, g, b):
    mu = jnp.mean(r, axis=-1, keepdims=True)
    d = r - mu
    var = jnp.mean(d * d, axis=-1, keepdims=True)
    return d * lax.rsqrt(var + LN_EPS) * g + b


def _inproj_kernel(x_ref, w_ref, o_ref):
    xb = x_ref[...].astype(BF16)
    for c0 in range(0, IN_COLS_PAD, INPROJ_COL_CHUNK):
        c1 = min(c0 + INPROJ_COL_CHUNK, IN_COLS_PAD)
        o_ref[:, c0:c1] = _dot(xb, w_ref[:, c0:c1])


def _resident(shape):
    return pl.BlockSpec(shape, lambda *_: (0,) * len(shape), pipeline_mode=pl.Buffered(1))


def _inproj(x2d, w_pad):
    t = x2d.shape[0]
    return pl.pallas_call(
        _inproj_kernel,
        out_shape=jax.ShapeDtypeStruct((t, IN_COLS_PAD), F32),
        grid=(t // ROW_TILE,),
        in_specs=[
            pl.BlockSpec((ROW_TILE, D_MODEL), lambda i: (i, 0)),
            _resident((D_MODEL, IN_COLS_PAD)),
        ],
        out_specs=pl.BlockSpec((ROW_TILE, IN_COLS_PAD), lambda i: (i, 0)),
        compiler_params=_params("parallel"),
        name="inproj",
    )(x2d, w_pad)


def _moba_kernel(slopes_ref, q_ref, k_ref, v_ref, o_ref, kb_ref, vt_ref, km_ref, al_ref, s_ref):
    hp = pl.program_id(1)
    nb = kb_ref.shape[0]
    blk = MOBA_BLOCK
    scale = HEAD_DIM ** -0.5

    for n in range(nb):
        kblk = k_ref[0, n * blk:(n + 1) * blk, :]
        kb_ref[n] = kblk.astype(BF16)
        km_ref[n:n + 1, :] = jnp.sum(kblk, axis=0, keepdims=True) * (1.0 / blk)
        vt_ref[n] = v_ref[0, n * blk:(n + 1) * blk, :].T.astype(BF16)

    lane = lax.broadcasted_iota(jnp.int32, (blk, LANES), 1)
    key_in = lax.broadcasted_iota(jnp.int32, (blk, blk), 0)
    qry_in = lax.broadcasted_iota(jnp.int32, (blk, blk), 1)
    dist0 = (qry_in - key_in).astype(F32)
    causal = qry_in >= key_in
    km = km_ref[...]
    km_hi = km.astype(BF16)
    km_lo = (km - km_hi.astype(F32)).astype(BF16)
    vt_rows = [slice(hd * HEAD_DIM, (hd + 1) * HEAD_DIM) for hd in range(2)]
    head_lanes = [(lane >= hd * HEAD_DIM) & (lane < (hd + 1) * HEAD_DIM) for hd in range(2)]
    slopes = [slopes_ref[2 * hp + hd] for hd in range(2)]
    for hd in range(2):
        al = -slopes[hd] * dist0
        al_ref[hd, 0] = al
        al_ref[hd, 1] = jnp.where(causal, al, MASK_NEG)

    for i in range(nb):
        q = q_ref[0, i * blk:(i + 1) * blk, :]
        outs = []
        for hd in range(2):
            slope = slopes[hd]
            q_f = jnp.where(head_lanes[hd], q, 0.0) * scale
            q_hi = q_f.astype(BF16)
            if i > MOBA_TOPK:
                q_lo = (q_f - q_hi.astype(F32)).astype(BF16)
                gate = (_dot_nt(km_hi, q_hi) + _dot_nt(km_lo, q_hi)
                        + _dot_nt(km_hi, q_lo))
                blk_id = lax.broadcasted_iota(jnp.int32, gate.shape, 0)
                rank = jnp.zeros(gate.shape, jnp.int32)
                for m in range(i):
                    gm = gate[m:m + 1, :]
                    beats = (gm > gate) | ((gm == gate) & (blk_id > m))
                    rank = rank + jnp.where(beats & (blk_id != m), 1, 0)
                blk_dist = ((i - blk_id) * blk).astype(F32)
                rb = jnp.where(rank < MOBA_TOPK, -slope * blk_dist, MASK_NEG)
                row_bias = [rb[j:j + 1, :] for j in range(i)]
            else:
                row_bias = [-slope * float((i - j) * blk) for j in range(i)]

            sbuf = s_ref.at[i % 2, hd]
            s = _dot_nt(kb_ref[i], q_hi) + al_ref[hd, 1]
            sbuf[i] = s
            m_max = jnp.max(s, axis=0, keepdims=True)
            for j in range(i):
                s = _dot_nt(kb_ref[j], q_hi) + al_ref[hd, 0] + row_bias[j]
                sbuf[j] = s
                m_max = jnp.maximum(m_max, jnp.max(s, axis=0, keepdims=True))
            l_sum = jnp.zeros((1, blk), F32)
            acc = jnp.zeros((HEAD_DIM, blk), F32)
            for j in range(i + 1):
                p = jnp.exp(sbuf[j] - m_max)
                l_sum = l_sum + jnp.sum(p, axis=0, keepdims=True)
                acc = acc + _dot(vt_ref[j, vt_rows[hd], :], p.astype(BF16))
            outs.append(acc / l_sum)
        o_ref[0, i * blk:(i + 1) * blk, :] = jnp.concatenate(outs, axis=0).T


def _moba(h3d, slopes):
    bsz, seq, _ = h3d.shape
    nb = seq // MOBA_BLOCK
    n_pairs = ATTN_WIDTH // LANES
    return pl.pallas_call(
        _moba_kernel,
        out_shape=jax.ShapeDtypeStruct((bsz, seq, ATTN_WIDTH), F32),
        grid=(bsz, n_pairs),
        in_specs=[
            pl.BlockSpec(memory_space=pltpu.SMEM),
            pl.BlockSpec((1, seq, LANES), lambda b, p: (b, 0, COL_Q // LANES + p)),
            pl.BlockSpec((1, seq, LANES), lambda b, p: (b, 0, COL_K // LANES + p)),
            pl.BlockSpec((1, seq, LANES), lambda b, p: (b, 0, COL_V // LANES + p)),
        ],
        out_specs=pl.BlockSpec((1, seq, LANES), lambda b, p: (b, 0, p)),
        scratch_shapes=[
            pltpu.VMEM((nb, MOBA_BLOCK, LANES), BF16),
            pltpu.VMEM((nb, LANES, MOBA_BLOCK), BF16),
            pltpu.VMEM((nb, LANES), F32),
            pltpu.VMEM((2, 2, MOBA_BLOCK, MOBA_BLOCK), F32),
            pltpu.VMEM((2, 2, nb, MOBA_BLOCK, MOBA_BLOCK), F32),
        ],
        compiler_params=_params("parallel", "parallel"),
        name="moba_attention",
    )(slopes, h3d, h3d, h3d)


def _ssd_kernel(xbc_ref, dt_ref, z_ref, cw_ref, cb_ref, dtb_ref, alog_ref, dskip_ref, nw_ref,
                y_ref, xpad_ref, hst_ref):
    c = pl.program_id(1)
    L = SSD_CHUNK
    gw = SSD_INNER // SSD_GROUPS

    @pl.when(c == 0)
    def _():
        xpad_ref[0:SSD_HALO, :] = jnp.zeros((SSD_HALO, SSD_CONV_CH), F32)
        hst_ref[...] = jnp.zeros(hst_ref.shape, F32)

    xpad_ref[SSD_HALO:SSD_HALO + L, :] = xbc_ref[0]
    conv = jnp.broadcast_to(cb_ref[...], (L, SSD_CONV_CH))
    for k in range(SSD_CONV):
        off = SSD_HALO - (SSD_CONV - 1) + k
        conv = conv + cw_ref[k:k + 1, :] * xpad_ref[off:off + L, :]
    xpad_ref[0:SSD_HALO, :] = xpad_ref[L:L + SSD_HALO, :]
    xbc = _silu(conv)
    xs = xbc[:, :SSD_INNER]

    dt_in = dt_ref[0] + dtb_ref[...]
    dt = jnp.maximum(dt_in, 0.0) + jnp.log1p(jnp.exp(-jnp.abs(dt_in)))
    a = -jnp.exp(alog_ref[...])
    a_dt = dt * a
    row = lax.broadcasted_iota(jnp.int32, (L, L), 0)
    col = lax.broadcasted_iota(jnp.int32, (L, L), 1)
    causal = row >= col
    tril = jnp.where(causal, 1.0, 0.0).astype(BF16)
    a_cs = _dot_exact_rhs_lhs(tril, a_dt)
    a_cs_t = a_cs.T
    a_last = a_cs[L - 1:L, :]
    decay_to_end = jnp.exp(a_last - a_cs)
    decay_from_start = jnp.exp(a_cs)

    eh = lax.broadcasted_iota(jnp.int32, (LANES, SSD_INNER), 0)
    ec = lax.broadcasted_iota(jnp.int32, (LANES, SSD_INNER), 1)
    expand = jnp.where((ec >= eh * HEAD_DIM) & (ec < (eh + 1) * HEAD_DIM), 1.0, 0.0).astype(BF16)
    dt_full = _dot_exact_rhs(dt, expand)
    dte_full = _dot_exact_rhs(decay_to_end, expand)
    dfs_full = _dot_exact_rhs(decay_from_start, expand)

    x_dt = xs * dt_full
    x_dte = (x_dt * dte_full).astype(BF16)
    x_dt_b = x_dt.astype(BF16)
    first_head = lax.broadcasted_iota(jnp.int32, (L, LANES), 1) < HEAD_DIM
    heads_per_group = SSD_HEADS // SSD_GROUPS

    for g in range(SSD_GROUPS):
        bm = xbc[:, SSD_INNER + g * SSD_STATE:SSD_INNER + (g + 1) * SSD_STATE]
        cm = xbc[:, SSD_INNER + (SSD_GROUPS + g) * SSD_STATE:SSD_INNER + (SSD_GROUPS + g + 1) * SSD_STATE]
        bm_b = bm.astype(BF16)
        cm_b = cm.astype(BF16)
        cbm = _dot_nt(cm_b, bm_b)
        gs = slice(g * gw, (g + 1) * gw)

        h_prev = hst_ref[g]
        y_g = _dot(cm_b, h_prev.astype(BF16)) * dfs_full[:, gs]
        st = _dot(bm.T.astype(BF16), x_dte[:, gs])
        hst_ref[g] = h_prev * dfs_full[L - 1:L, gs] + st

        pieces = []
        for pi in range(heads_per_group // 2):
            ls = slice(g * gw + pi * LANES, g * gw + (pi + 1) * LANES)
            xp = x_dt_b[:, ls]
            yd = []
            for hd in range(2):
                h = g * heads_per_group + 2 * pi + hd
                seg = a_cs[:, h:h + 1] - a_cs_t[h:h + 1, :]
                dec = jnp.exp(jnp.where(causal, seg, MASK_NEG))
                yd.append(_dot((cbm * dec).astype(BF16), xp))
            pieces.append(jnp.where(first_head, yd[0], yd[1]))
        y_g = y_g + jnp.concatenate(pieces, axis=1) + xs[:, gs] * dskip_ref[:, gs]

        gz = y_g * _silu(z_ref[0, :, gs])
        ms = jnp.mean(gz * gz, axis=-1, keepdims=True)
        y_ref[0, :, gs] = gz * lax.rsqrt(ms + RMS_EPS) * nw_ref[:, gs]


def _dot_exact_rhs_lhs(lhs_bf16, x):
    hi, mid, lo = _split3(x)
    return _dot(lhs_bf16, hi) + _dot(lhs_bf16, mid) + _dot(lhs_bf16, lo)


def _ssd(h3d, conv_w, conv_b, dt_bias, a_log, d_full, norm_w):
    bsz, seq, _ = h3d.shape
    n_c = seq // SSD_CHUNK
    const = lambda b, c: (0, 0)
    return pl.pallas_call(
        _ssd_kernel,
        out_shape=jax.ShapeDtypeStruct((bsz, seq, SSD_INNER), F32),
        grid=(bsz, n_c),
        in_specs=[
            pl.BlockSpec((1, SSD_CHUNK, SSD_CONV_CH), lambda b, c: (b, c, COL_XBC // SSD_CONV_CH)),
            pl.BlockSpec((1, SSD_CHUNK, LANES), lambda b, c: (b, c, COL_DT // LANES)),
            pl.BlockSpec((1, SSD_CHUNK, SSD_INNER), lambda b, c: (b, c, COL_Z // SSD_INNER)),
            pl.BlockSpec((SSD_CONV, SSD_CONV_CH), const),
            pl.BlockSpec((1, SSD_CONV_CH), const),
            pl.BlockSpec((1, LANES), const),
            pl.BlockSpec((1, LANES), const),
            pl.BlockSpec((1, SSD_INNER), const),
            pl.BlockSpec((1, SSD_INNER), const),
        ],
        out_specs=pl.BlockSpec((1, SSD_CHUNK, SSD_INNER), lambda b, c: (b, c, 0)),
        scratch_shapes=[
            pltpu.VMEM((SSD_HALO + SSD_CHUNK, SSD_CONV_CH), F32),
            pltpu.VMEM((SSD_GROUPS, SSD_STATE, SSD_INNER // SSD_GROUPS), F32),
        ],
        compiler_params=_params("parallel", "arbitrary"),
        name="ssd_scan",
    )(h3d, h3d, h3d, conv_w, conv_b, dt_bias, a_log, d_full, norm_w)


def _outproj_ln_kernel(attn_ref, y_ref, x_ref, w_ref, g_ref, b_ref, o_ref):
    mixed = (_dot(attn_ref[...].astype(BF16), w_ref[0:ATTN_WIDTH, :])
             + _dot(y_ref[...].astype(BF16), w_ref[ATTN_WIDTH:, :]))
    o_ref[...] = _layer_norm(DN_ALPHA * x_ref[...] + mixed, g_ref[...], b_ref[...])


def _outproj_ln(attn2d, y2d, x2d, w_out, g, b):
    t = x2d.shape[0]
    const = lambda i: (0, 0)
    return pl.pallas_call(
        _outproj_ln_kernel,
        out_shape=jax.ShapeDtypeStruct((t, D_MODEL), F32),
        grid=(t // ROW_TILE,),
        in_specs=[
            pl.BlockSpec((ROW_TILE, ATTN_WIDTH), lambda i: (i, 0)),
            pl.BlockSpec((ROW_TILE, SSD_INNER), lambda i: (i, 0)),
            pl.BlockSpec((ROW_TILE, D_MODEL), lambda i: (i, 0)),
            pl.BlockSpec((ATTN_WIDTH + SSD_INNER, D_MODEL), const),
            pl.BlockSpec((1, D_MODEL), const),
            pl.BlockSpec((1, D_MODEL), const),
        ],
        out_specs=pl.BlockSpec((ROW_TILE, D_MODEL), lambda i: (i, 0)),
        compiler_params=_params("parallel"),
        name="outproj_ln",
    )(attn2d, y2d, x2d, w_out, g, b)


def _ffn_ln_kernel(x_ref, wg_ref, wu_ref, wd_ref, g_ref, b_ref, o_ref):
    x = x_ref[...]
    xb = x.astype(BF16)
    gate = _dot(xb, wg_ref[...])
    up = _dot(xb, wu_ref[...])
    ffn = _dot((_silu(gate) * up).astype(BF16), wd_ref[...])
    o_ref[...] = _layer_norm(DN_ALPHA * x + ffn, g_ref[...], b_ref[...])


def _ffn_ln(x2d, w_gate, w_up, w_down, g, b):
    t = x2d.shape[0]
    hid = w_gate.shape[1]
    return pl.pallas_call(
        _ffn_ln_kernel,
        out_shape=jax.ShapeDtypeStruct((t, D_MODEL), F32),
        grid=(t // ROW_TILE,),
        in_specs=[
            pl.BlockSpec((ROW_TILE, D_MODEL), lambda i: (i, 0)),
            _resident((D_MODEL, hid)),
            _resident((D_MODEL, hid)),
            _resident((hid, D_MODEL)),
            _resident((1, D_MODEL)),
            _resident((1, D_MODEL)),
        ],
        out_specs=pl.BlockSpec((ROW_TILE, D_MODEL), lambda i: (i, 0)),
        compiler_params=_params("parallel"),
        name="ffn_ln",
    )(x2d, w_gate, w_up, w_down, g, b)


def _glu_kernel(x_ref, wv_ref, wg_ref, bv_ref, bg_ref, o_ref):
    xb = x_ref[...].astype(BF16)
    val = _dot(xb, wv_ref[...]) + bv_ref[...]
    gate = _dot(xb, wg_ref[...]) + bg_ref[...]
    o_ref[...] = val * jax.nn.sigmoid(gate)


def _pw1_glu(x2d, w_pw1, b_pw1):
    t = x2d.shape[0]
    n_col = D_MODEL // GLU_COL_TILE
    return pl.pallas_call(
        _glu_kernel,
        out_shape=jax.ShapeDtypeStruct((t, D_MODEL), F32),
        grid=(t // ROW_TILE, n_col),
        in_specs=[
            pl.BlockSpec((ROW_TILE, D_MODEL), lambda i, j: (i, 0)),
            pl.BlockSpec((D_MODEL, GLU_COL_TILE), lambda i, j: (0, j)),
            pl.BlockSpec((D_MODEL, GLU_COL_TILE), lambda i, j: (0, n_col + j)),
            pl.BlockSpec((1, GLU_COL_TILE), lambda i, j: (0, j)),
            pl.BlockSpec((1, GLU_COL_TILE), lambda i, j: (0, n_col + j)),
        ],
        out_specs=pl.BlockSpec((ROW_TILE, GLU_COL_TILE), lambda i, j: (i, j)),
        compiler_params=_params("parallel", "arbitrary"),
        name="pw1_glu",
    )(x2d, w_pw1, w_pw1, b_pw1, b_pw1)


def _conv_pw2_ln_kernel(h_ref, x_ref, dww_ref, dwb_ref, cg_ref, cb_ref, w2_ref, b2_ref,
                        g_ref, b_ref, o_ref, hpad_ref, cv_ref):
    t = pl.program_id(1)
    ts = CONV_SEQ_TILE

    @pl.when(t == 0)
    def _():
        hpad_ref[0:CONV_HALO, :] = jnp.zeros((CONV_HALO, D_MODEL), F32)

    hpad_ref[CONV_HALO:CONV_HALO + ts, :] = h_ref[0]
    base = CONV_HALO - (CONF_KERNEL - 1)

    def row_chunk(rc, carry):
        r0 = pl.multiple_of(rc * CONV_ROW_CHUNK, CONV_ROW_CHUNK)
        for cc in range(D_MODEL // CONV_COL_CHUNK):
            cs = slice(cc * CONV_COL_CHUNK, (cc + 1) * CONV_COL_CHUNK)
            acc = jnp.broadcast_to(dwb_ref[:, cs], (CONV_ROW_CHUNK, CONV_COL_CHUNK))
            window = hpad_ref.at[pl.ds(r0, CONV_ROW_CHUNK + CONV_HALO), cs]
            for k in range(CONF_KERNEL):
                acc = acc + dww_ref[k:k + 1, cs] * window[base + k:base + k + CONV_ROW_CHUNK, :]
            cv_ref[pl.ds(r0, CONV_ROW_CHUNK), cs] = acc
        return carry

    lax.fori_loop(0, ts // CONV_ROW_CHUNK, row_chunk, 0)
    hpad_ref[0:CONV_HALO, :] = hpad_ref[ts:ts + CONV_HALO, :]

    hn = _silu(_layer_norm(cv_ref[...], cg_ref[...], cb_ref[...]))
    mixed = _dot(hn.astype(BF16), w2_ref[...]) + b2_ref[...]
    o_ref[0] = _layer_norm(DN_ALPHA * x_ref[0] + mixed, g_ref[...], b_ref[...])


def _conv_pw2_ln(h3d, x3d, dw_w, dw_b, cg, cb, w2, b2, g, b):
    bsz, seq, _ = x3d.shape
    const = lambda bb, tt: (0, 0)
    vec = pl.BlockSpec((1, D_MODEL), const)
    return pl.pallas_call(
        _conv_pw2_ln_kernel,
        out_shape=jax.ShapeDtypeStruct((bsz, seq, D_MODEL), F32),
        grid=(bsz, seq // CONV_SEQ_TILE),
        in_specs=[
            pl.BlockSpec((1, CONV_SEQ_TILE, D_MODEL), lambda bb, tt: (bb, tt, 0)),
            pl.BlockSpec((1, CONV_SEQ_TILE, D_MODEL), lambda bb, tt: (bb, tt, 0)),
            pl.BlockSpec((CONV_HALO, D_MODEL), const),
            vec, vec, vec,
            pl.BlockSpec((D_MODEL, D_MODEL), const),
            vec, vec, vec,
        ],
        out_specs=pl.BlockSpec((1, CONV_SEQ_TILE, D_MODEL), lambda bb, tt: (bb, tt, 0)),
        scratch_shapes=[
            pltpu.VMEM((CONV_HALO + CONV_SEQ_TILE, D_MODEL), F32),
            pltpu.VMEM((CONV_SEQ_TILE, D_MODEL), F32),
        ],
        compiler_params=_params("parallel", "arbitrary"),
        name="conv_pw2_ln",
    )(h3d, x3d, dw_w, dw_b, cg, cb, w2, b2, g, b)


def _row(v):
    return v.reshape(1, -1).astype(F32)


def _pad_lanes(v):
    return jnp.pad(v.reshape(1, -1).astype(F32), ((0, 0), (0, LANES - v.shape[-1])))


def kernel(x, mix_w_in, ssd_conv_w, ssd_conv_b, ssd_dt_bias, ssd_a_log, ssd_d, ssd_norm_w, mix_w_out, conv_w_pw1, conv_b_pw1, conv_dw_w, conv_dw_b, conv_ln_g, conv_ln_b, conv_w_pw2, conv_b_pw2, ffn_w_gate, ffn_w_up, ffn_w_down, ln_mix_g, ln_mix_b, ln_ffn_g, ln_ffn_b):
    bsz, seq, d = x.shape
    t = bsz * seq
    n_heads = ATTN_WIDTH // HEAD_DIM
    slopes = 2.0 ** (-8.0 * jnp.arange(1, n_heads + 1, dtype=F32) / n_heads)

    w_in = jnp.pad(mix_w_in[0], ((0, 0), (0, IN_COLS_PAD - mix_w_in.shape[-1]))).astype(BF16)
    h = _inproj(x.reshape(t, d), w_in).reshape(bsz, seq, IN_COLS_PAD)
    attn = _moba(h, slopes)
    y = _ssd(h, ssd_conv_w[0], _row(ssd_conv_b[0]), _pad_lanes(ssd_dt_bias[0]), _pad_lanes(ssd_a_log[0]),
             _row(jnp.repeat(ssd_d[0], HEAD_DIM)), _row(ssd_norm_w[0]))
    x1 = _outproj_ln(attn.reshape(t, ATTN_WIDTH), y.reshape(t, SSD_INNER), x.reshape(t, d),
                     mix_w_out[0].astype(BF16), _row(ln_mix_g[0]), _row(ln_mix_b[0]))
    x2 = _ffn_ln(x1, ffn_w_gate[0].astype(BF16), ffn_w_up[0].astype(BF16), ffn_w_down[0].astype(BF16),
                 _row(ln_ffn_g[0]), _row(ln_ffn_b[0]))

    hc = _pw1_glu(x2, conv_w_pw1[0].astype(BF16), _row(conv_b_pw1[0]))
    dw_w = jnp.pad(conv_dw_w[0], ((0, CONV_HALO - CONF_KERNEL), (0, 0)))
    x3 = _conv_pw2_ln(hc.reshape(bsz, seq, d), x2.reshape(bsz, seq, d), dw_w, _row(conv_dw_b[0]),
                      _row(conv_ln_g[0]), _row(conv_ln_b[0]), conv_w_pw2[0].astype(BF16),
                      _row(conv_b_pw2[0]), _row(ln_mix_g[1]), _row(ln_mix_b[1]))
    x4 = _ffn_ln(x3.reshape(t, d), ffn_w_gate[1].astype(BF16), ffn_w_up[1].astype(BF16),
                 ffn_w_down[1].astype(BF16), _row(ln_ffn_g[1]), _row(ln_ffn_b[1]))
    return x4.reshape(bsz, seq, d)
```

```python
import functools

import jax
import jax.numpy as jnp
from jax import lax
from jax.experimental import pallas as pl
from jax.experimental.pallas import tpu as pltpu

F32 = jnp.float32
BF16 = jnp.bfloat16

D_MODEL = 1024
DEPTH = 2
ATTN_WIDTH = 512
HEAD_DIM = 64
MOBA_BLOCK = 256
MOBA_TOPK = 3
SSD_INNER = 512
SSD_HEADS = 8
SSD_GROUPS = 2
SSD_STATE = 128
SSD_CONV = 4
SSD_CHUNK = 256
SSD_CONV_CH = SSD_INNER + 2 * SSD_GROUPS * SSD_STATE
CONF_KERNEL = 31
DN_ALPHA = (2 * DEPTH) ** 0.25
LN_EPS = 1e-5
RMS_EPS = 1e-5

LANES = 128
SUBLANES = 8

COL_Q = 0
COL_K = ATTN_WIDTH
COL_V = 2 * ATTN_WIDTH
COL_Z = 3 * ATTN_WIDTH
COL_XBC = COL_Z + SSD_INNER
COL_DT = COL_XBC + SSD_CONV_CH
IN_COLS_PAD = COL_DT + LANES

MASK_NEG = -1e30

ROW_TILE = 512
INPROJ_COL_CHUNK = 512
CONV_SEQ_TILE = 512
CONV_HALO = 32
CONV_ROW_CHUNK = 128
SSD_HALO = 8
VMEM_LIMIT = 56 * 1024 * 1024


def _params(*sem):
    return pltpu.CompilerParams(dimension_semantics=sem, vmem_limit_bytes=VMEM_LIMIT)


def _dot(a, b):
    return jnp.dot(a, b, preferred_element_type=F32)


def _dot_nt(a, b):
    return lax.dot_general(a, b, (((1,), (1,)), ((), ())), preferred_element_type=F32)


def _split3(x):
    hi = x.astype(BF16)
    r1 = x - hi.astype(F32)
    mid = r1.astype(BF16)
    lo = (r1 - mid.astype(F32)).astype(BF16)
    return hi, mid, lo


def _dot_exact_rhs(x, rhs_bf16):
    hi, mid, lo = _split3(x)
    return _dot(hi, rhs_bf16) + _dot(mid, rhs_bf16) + _dot(lo, rhs_bf16)


def _silu(x):
    return x * jax.nn.sigmoid(x)


def _layer_norm(r, g, b):
    mu = jnp.mean(r, axis=-1, keepdims=True)
    d = r - mu
    var = jnp.mean(d * d, axis=-1, keepdims=True)
    return d * lax.rsqrt(var + LN_EPS) * g + b


def _inproj_kernel(x_ref, w_ref, o_ref):
    xb = x_ref[...].astype(BF16)
    for c0 in range(0, IN_COLS_PAD, INPROJ_COL_CHUNK):
        c1 = min(c0 + INPROJ_COL_CHUNK, IN_COLS_PAD)
        o_ref[:, c0:c1] = _dot(xb, w_ref[:, c0:c1])


def _resident(shape):
    return pl.BlockSpec(shape, lambda *_: (0,) * len(shape), pipeline_mode=pl.Buffered(1))


def _inproj(x2d, w_pad):
    t = x2d.shape[0]
    return pl.pallas_call(
        _inproj_kernel,
        out_shape=jax.ShapeDtypeStruct((t, IN_COLS_PAD), F32),
        grid=(t // ROW_TILE,),
        in_specs=[
            pl.BlockSpec((ROW_TILE, D_MODEL), lambda i: (i, 0)),
            _resident((D_MODEL, IN_COLS_PAD)),
        ],
        out_specs=pl.BlockSpec((ROW_TILE, IN_COLS_PAD), lambda i: (i, 0)),
        compiler_params=_params("parallel"),
        name="inproj",
    )(x2d, w_pad)


def _moba_kernel(slopes_ref, q_ref, k_ref, v_ref, o_ref, kb_ref, vt_ref, km_ref, al_ref, s_ref):
    hp = pl.program_id(1)
    nb = kb_ref.shape[0]
    blk = MOBA_BLOCK
    scale = HEAD_DIM ** -0.5

    for n in range(nb):
        kblk = k_ref[0, n * blk:(n + 1) * blk, :]
        kb_ref[n] = kblk.astype(BF16)
        km_ref[n:n + 1, :] = jnp.sum(kblk, axis=0, keepdims=True) * (1.0 / blk)
        vt_ref[n] = v_ref[0, n * blk:(n + 1) * blk, :].T.astype(BF16)

    lane = lax.broadcasted_iota(jnp.int32, (blk, LANES), 1)
    key_in = lax.broadcasted_iota(jnp.int32, (blk, blk), 0)
    qry_in = lax.broadcasted_iota(jnp.int32, (blk, blk), 1)
    dist0 = (qry_in - key_in).astype(F32)
    causal = qry_in >= key_in
    km = km_ref[...]
    km_hi = km.astype(BF16)
    km_lo = (km - km_hi.astype(F32)).astype(BF16)
    vt_rows = [slice(hd * HEAD_DIM, (hd + 1) * HEAD_DIM) for hd in range(2)]
    head_lanes = [(lane >= hd * HEAD_DIM) & (lane < (hd + 1) * HEAD_DIM) for hd in range(2)]
    slopes = [slopes_ref[2 * hp + hd] for hd in range(2)]
    for hd in range(2):
        al = -slopes[hd] * dist0
        al_ref[hd, 0] = al
        al_ref[hd, 1] = jnp.where(causal, al, MASK_NEG)

    for i in range(nb):
        q = q_ref[0, i * blk:(i + 1) * blk, :]
        outs = []
        for hd in range(2):
            slope = slopes[hd]
            q_f = jnp.where(head_lanes[hd], q, 0.0) * scale
            q_hi = q_f.astype(BF16)
            if i > MOBA_TOPK:
                q_lo = (q_f - q_hi.astype(F32)).astype(BF16)
                gate = (_dot_nt(km_hi, q_hi) + _dot_nt(km_lo, q_hi)
                        + _dot_nt(km_hi, q_lo))
                blk_id = lax.broadcasted_iota(jnp.int32, gate.shape, 0)
                rank = jnp.zeros(gate.shape, jnp.int32)
                for m in range(i):
                    gm = gate[m:m + 1, :]
                    beats = (gm > gate) | ((gm == gate) & (blk_id > m))
                    rank = rank + jnp.where(beats & (blk_id != m), 1, 0)
                blk_dist = ((i - blk_id) * blk).astype(F32)
                rb = jnp.where(rank < MOBA_TOPK, -slope * blk_dist, MASK_NEG)
                row_bias = [rb[j:j + 1, :] for j in range(i)]
            else:
                row_bias = [-slope * float((i - j) * blk) for j in range(i)]

            sbuf = s_ref.at[i % 2, hd]
            s = _dot_nt(kb_ref[i], q_hi) + al_ref[hd, 1]
            sbuf[i] = s
            m_max = jnp.max(s, axis=0, keepdims=True)
            for j in range(i):
                s = _dot_nt(kb_ref[j], q_hi) + al_ref[hd, 0] + row_bias[j]
                sbuf[j] = s
                m_max = jnp.maximum(m_max, jnp.max(s, axis=0, keepdims=True))
            l_sum = jnp.zeros((1, blk), F32)
            acc = jnp.zeros((HEAD_DIM, blk), F32)
            for j in range(i + 1):
                p = jnp.exp(sbuf[j] - m_max)
                l_sum = l_sum + jnp.sum(p, axis=0, keepdims=True)
                acc = acc + _dot(vt_ref[j, vt_rows[hd], :], p.astype(BF16))
            outs.append(acc / l_sum)
        o_ref[0, i * blk:(i + 1) * blk, :] = jnp.concatenate(outs, axis=0).T


def _moba(h3d, slopes):
    bsz, seq, _ = h3d.shape
    nb = seq // MOBA_BLOCK
    n_pairs = ATTN_WIDTH // LANES
    return pl.pallas_call(
        _moba_kernel,
        out_shape=jax.ShapeDtypeStruct((bsz, seq, ATTN_WIDTH), F32),
        grid=(bsz, n_pairs),
        in_specs=[
            pl.BlockSpec(memory_space=pltpu.SMEM),
            pl.BlockSpec((1, seq, LANES), lambda b, p: (b, 0, COL_Q // LANES + p)),
            pl.BlockSpec((1, seq, LANES), lambda b, p: (b, 0, COL_K // LANES + p)),
            pl.BlockSpec((1, seq, LANES), lambda b, p: (b, 0, COL_V // LANES + p)),
        ],
        out_specs=pl.BlockSpec((1, seq, LANES), lambda b, p: (b, 0, p)),
        scratch_shapes=[
            pltpu.VMEM((nb, MOBA_BLOCK, LANES), BF16),
            pltpu.VMEM((nb, LANES, MOBA_BLOCK), BF16),
            pltpu.VMEM((nb, LANES), F32),
            pltpu.VMEM((2, 2, MOBA_BLOCK, MOBA_BLOCK), F32),
            pltpu.VMEM((2, 2, nb, MOBA_BLOCK, MOBA_BLOCK), F32),
        ],
        compiler_params=_params("parallel", "parallel"),
        name="moba_attention",
    )(slopes, h3d, h3d, h3d)


def _ssd_kernel(xbc_ref, dt_ref, z_ref, cw_ref, cb_ref, dtb_ref, alog_ref, dskip_ref, nw_ref,
                y_ref, xpad_ref, hst_ref):
    c = pl.program_id(1)
    L = SSD_CHUNK
    gw = SSD_INNER // SSD_GROUPS

    @pl.when(c == 0)
    def _():
        xpad_ref[0:SSD_HALO, :] = jnp.zeros((SSD_HALO, SSD_CONV_CH), F32)
        hst_ref[...] = jnp.zeros(hst_ref.shape, F32)

    xpad_ref[SSD_HALO:SSD_HALO + L, :] = xbc_ref[0]
    conv = jnp.broadcast_to(cb_ref[...], (L, SSD_CONV_CH))
    for k in range(SSD_CONV):
        off = SSD_HALO - (SSD_CONV - 1) + k
        conv = conv + cw_ref[k:k + 1, :] * xpad_ref[off:off + L, :]
    xpad_ref[0:SSD_HALO, :] = xpad_ref[L:L + SSD_HALO, :]
    xbc = _silu(conv)
    xs = xbc[:, :SSD_INNER]

    dt_in = dt_ref[0] + dtb_ref[...]
    dt = jnp.maximum(dt_in, 0.0) + jnp.log1p(jnp.exp(-jnp.abs(dt_in)))
    a = -jnp.exp(alog_ref[...])
    a_dt = dt * a
    row = lax.broadcasted_iota(jnp.int32, (L, L), 0)
    col = lax.broadcasted_iota(jnp.int32, (L, L), 1)
    causal = row >= col
    tril = jnp.where(causal, 1.0, 0.0).astype(BF16)
    a_cs = _dot_exact_rhs_lhs(tril, a_dt)
    a_cs_t = a_cs.T
    a_last = a_cs[L - 1:L, :]
    decay_to_end = jnp.exp(a_last - a_cs)
    decay_from_start = jnp.exp(a_cs)

    eh = lax.broadcasted_iota(jnp.int32, (LANES, SSD_INNER), 0)
    ec = lax.broadcasted_iota(jnp.int32, (LANES, SSD_INNER), 1)
    expand = jnp.where((ec >= eh * HEAD_DIM) & (ec < (eh + 1) * HEAD_DIM), 1.0, 0.0).astype(BF16)
    dt_full = _dot_exact_rhs(dt, expand)
    dte_full = _dot_exact_rhs(decay_to_end, expand)
    dfs_full = _dot_exact_rhs(decay_from_start, expand)

    x_dt = xs * dt_full
    x_dte = (x_dt * dte_full).astype(BF16)
    x_dt_b = x_dt.astype(BF16)
    first_head = lax.broadcasted_iota(jnp.int32, (L, LANES), 1) < HEAD_DIM
    heads_per_group = SSD_HEADS // SSD_GROUPS

    for g in range(SSD_GROUPS):
        bm = xbc[:, SSD_INNER + g * SSD_STATE:SSD_INNER + (g + 1) * SSD_STATE]
        cm = xbc[:, SSD_INNER + (SSD_GROUPS + g) * SSD_STATE:SSD_INNER + (SSD_GROUPS + g + 1) * SSD_STATE]
        bm_b = bm.astype(BF16)
        cm_b = cm.astype(BF16)
        cbm = _dot_nt(cm_b, bm_b)
        gs = slice(g * gw, (g + 1) * gw)

        h_prev = hst_ref[g]
        y_g = _dot(cm_b, h_prev.astype(BF16)) * dfs_full[:, gs]
        st = _dot(bm.T.astype(BF16), x_dte[:, gs])
        hst_ref[g] = h_prev * dfs_full[L - 1:L, gs] + st

        pieces = []
        for pi in range(heads_per_group // 2):
            ls = slice(g * gw + pi * LANES, g * gw + (pi + 1) * LANES)
            xp = x_dt_b[:, ls]
            yd = []
            for hd in range(2):
                h = g * heads_per_group + 2 * pi + hd
                seg = a_cs[:, h:h + 1] - a_cs_t[h:h + 1, :]
                dec = jnp.exp(jnp.where(causal, seg, MASK_NEG))
                yd.append(_dot((cbm * dec).astype(BF16), xp))
            pieces.append(jnp.where(first_head, yd[0], yd[1]))
        y_g = y_g + jnp.concatenate(pieces, axis=1) + xs[:, gs] * dskip_ref[:, gs]

        gz = y_g * _silu(z_ref[0, :, gs])
        ms = jnp.mean(gz * gz, axis=-1, keepdims=True)
        y_ref[0, :, gs] = gz * lax.rsqrt(ms + RMS_EPS) * nw_ref[:, gs]


def _dot_exact_rhs_lhs(lhs_bf16, x):
    hi, mid, lo = _split3(x)
    return _dot(lhs_bf16, hi) + _dot(lhs_bf16, mid) + _dot(lhs_bf16, lo)


def _ssd(h3d, conv_w, conv_b, dt_bias, a_log, d_full, norm_w):
    bsz, seq, _ = h3d.shape
    n_c = seq // SSD_CHUNK
    const = lambda b, c: (0, 0)
    return pl.pallas_call(
        _ssd_kernel,
        out_shape=jax.ShapeDtypeStruct((bsz, seq, SSD_INNER), F32),
        grid=(bsz, n_c),
        in_specs=[
            pl.BlockSpec((1, SSD_CHUNK, SSD_CONV_CH), lambda b, c: (b, c, COL_XBC // SSD_CONV_CH)),
            pl.BlockSpec((1, SSD_CHUNK, LANES), lambda b, c: (b, c, COL_DT // LANES)),
            pl.BlockSpec((1, SSD_CHUNK, SSD_INNER), lambda b, c: (b, c, COL_Z // SSD_INNER)),
            pl.BlockSpec((SSD_CONV, SSD_CONV_CH), const),
            pl.BlockSpec((1, SSD_CONV_CH), const),
            pl.BlockSpec((1, LANES), const),
            pl.BlockSpec((1, LANES), const),
            pl.BlockSpec((1, SSD_INNER), const),
            pl.BlockSpec((1, SSD_INNER), const),
        ],
        out_specs=pl.BlockSpec((1, SSD_CHUNK, SSD_INNER), lambda b, c: (b, c, 0)),
        scratch_shapes=[
            pltpu.VMEM((SSD_HALO + SSD_CHUNK, SSD_CONV_CH), F32),
            pltpu.VMEM((SSD_GROUPS, SSD_STATE, SSD_INNER // SSD_GROUPS), F32),
        ],
        compiler_params=_params("parallel", "arbitrary"),
        name="ssd_scan",
    )(h3d, h3d, h3d, conv_w, conv_b, dt_bias, a_log, d_full, norm_w)


def _ffn_ln_value(x, wg_ref, wu_ref, wd_ref, g_ref, b_ref):
    xb = x.astype(BF16)
    gate = _dot(xb, wg_ref[...])
    up = _dot(xb, wu_ref[...])
    ffn = _dot((_silu(gate) * up).astype(BF16), wd_ref[...])
    return _layer_norm(DN_ALPHA * x + ffn, g_ref[...], b_ref[...])


def _ffn_specs(hid):
    return [_resident((D_MODEL, hid)), _resident((D_MODEL, hid)), _resident((hid, D_MODEL)),
            _resident((1, D_MODEL)), _resident((1, D_MODEL))]


def _ffn_ln_kernel(x_ref, wg_ref, wu_ref, wd_ref, g_ref, b_ref, o_ref):
    o_ref[...] = _ffn_ln_value(x_ref[...], wg_ref, wu_ref, wd_ref, g_ref, b_ref)


def _ffn_ln(x2d, w_gate, w_up, w_down, g, b):
    t = x2d.shape[0]
    return pl.pallas_call(
        _ffn_ln_kernel,
        out_shape=jax.ShapeDtypeStruct((t, D_MODEL), F32),
        grid=(t // ROW_TILE,),
        in_specs=[pl.BlockSpec((ROW_TILE, D_MODEL), lambda i: (i, 0))] + _ffn_specs(w_gate.shape[1]),
        out_specs=pl.BlockSpec((ROW_TILE, D_MODEL), lambda i: (i, 0)),
        compiler_params=_params("parallel"),
        name="ffn_ln",
    )(x2d, w_gate, w_up, w_down, g, b)


def _mix_ffn_kernel(attn_ref, y_ref, x_ref, wo_ref, g1_ref, b1_ref,
                    wg_ref, wu_ref, wd_ref, g2_ref, b2_ref, o_ref):
    mixed = (_dot(attn_ref[...].astype(BF16), wo_ref[0:ATTN_WIDTH, :])
             + _dot(y_ref[...].astype(BF16), wo_ref[ATTN_WIDTH:, :]))
    x1 = _layer_norm(DN_ALPHA * x_ref[...] + mixed, g1_ref[...], b1_ref[...])
    o_ref[...] = _ffn_ln_value(x1, wg_ref, wu_ref, wd_ref, g2_ref, b2_ref)


def _mix_ffn(attn2d, y2d, x2d, w_out, g1, b1, w_gate, w_up, w_down, g2, b2):
    t = x2d.shape[0]
    return pl.pallas_call(
        _mix_ffn_kernel,
        out_shape=jax.ShapeDtypeStruct((t, D_MODEL), F32),
        grid=(t // ROW_TILE,),
        in_specs=[
            pl.BlockSpec((ROW_TILE, ATTN_WIDTH), lambda i: (i, 0)),
            pl.BlockSpec((ROW_TILE, SSD_INNER), lambda i: (i, 0)),
            pl.BlockSpec((ROW_TILE, D_MODEL), lambda i: (i, 0)),
            _resident((ATTN_WIDTH + SSD_INNER, D_MODEL)),
            _resident((1, D_MODEL)),
            _resident((1, D_MODEL)),
        ] + _ffn_specs(w_gate.shape[1]),
        out_specs=pl.BlockSpec((ROW_TILE, D_MODEL), lambda i: (i, 0)),
        compiler_params=_params("parallel"),
        name="outproj_ffn_ln",
    )(attn2d, y2d, x2d, w_out, g1, b1, w_gate, w_up, w_down, g2, b2)


def _conformer_kernel(x_ref, w1_ref, b1_ref, dww_ref, dwb_ref, cg_ref, cb_ref, w2_ref, b2_ref,
                      g_ref, b_ref, o_ref, hpad_ref, shift_ref, cv_ref):
    t = pl.program_id(1)
    ts = CONV_SEQ_TILE
    lead = CONV_HALO - SUBLANES

    n_slabs = D_MODEL // LANES

    @pl.when(t == 0)
    def _():
        hpad_ref[:, 0:CONV_HALO, :] = jnp.zeros((n_slabs, CONV_HALO, LANES), F32)

    x = x_ref[0]
    xb = x.astype(BF16)
    val = _dot(xb, w1_ref[:, :D_MODEL]) + b1_ref[:, :D_MODEL]
    gate = _dot(xb, w1_ref[:, D_MODEL:]) + b1_ref[:, D_MODEL:]
    h = val * jax.nn.sigmoid(gate)
    for cc in range(n_slabs):
        hpad_ref[cc, CONV_HALO:CONV_HALO + ts, :] = h[:, cc * LANES:(cc + 1) * LANES]
    for r in range(1, SUBLANES):
        shift_ref[r - 1] = hpad_ref[:, SUBLANES - r:SUBLANES - r + lead + ts, :]

    tiles = CONV_ROW_CHUNK // SUBLANES
    for cc in range(n_slabs):
        cs = slice(cc * LANES, (cc + 1) * LANES)
        taps = [jnp.broadcast_to(dww_ref[k:k + 1, cs], (SUBLANES, LANES)) for k in range(CONF_KERNEL)]
        bias = jnp.broadcast_to(dwb_ref[:, cs], (SUBLANES, LANES))

        def row_chunk(rc, carry, cc=cc, taps=taps, bias=bias):
            r0 = pl.multiple_of(rc * CONV_ROW_CHUNK, CONV_ROW_CHUNK)
            acc = jnp.broadcast_to(bias[None], (tiles, SUBLANES, LANES))
            for k in range(CONF_KERNEL):
                a, r = divmod(CONF_KERNEL - 1 - k, SUBLANES)
                if r == 0:
                    src = hpad_ref[cc, pl.ds(r0 + CONV_HALO - SUBLANES * a, CONV_ROW_CHUNK), :]
                else:
                    src = shift_ref[r - 1, cc, pl.ds(r0 + lead - SUBLANES * a, CONV_ROW_CHUNK), :]
                acc = acc + taps[k][None] * src.reshape(tiles, SUBLANES, LANES)
            cv_ref[cc, pl.ds(r0, CONV_ROW_CHUNK), :] = acc.reshape(CONV_ROW_CHUNK, LANES)
            return carry

        lax.fori_loop(0, ts // CONV_ROW_CHUNK, row_chunk, 0)
    hpad_ref[:, 0:CONV_HALO, :] = hpad_ref[:, ts:ts + CONV_HALO, :]

    conv = jnp.concatenate([cv_ref[cc] for cc in range(n_slabs)], axis=1)
    hn = _silu(_layer_norm(conv, cg_ref[...], cb_ref[...]))
    mixed = _dot(hn.astype(BF16), w2_ref[...]) + b2_ref[...]
    o_ref[0] = _layer_norm(DN_ALPHA * x + mixed, g_ref[...], b_ref[...])


def _conformer(x3d, w1, b1, dw_w, dw_b, cg, cb, w2, b2, g, b):
    bsz, seq, _ = x3d.shape
    vec = _resident((1, D_MODEL))
    return pl.pallas_call(
        _conformer_kernel,
        out_shape=jax.ShapeDtypeStruct((bsz, seq, D_MODEL), F32),
        grid=(bsz, seq // CONV_SEQ_TILE),
        in_specs=[
            pl.BlockSpec((1, CONV_SEQ_TILE, D_MODEL), lambda bb, tt: (bb, tt, 0)),
            _resident((D_MODEL, 2 * D_MODEL)),
            _resident((1, 2 * D_MODEL)),
            _resident((CONV_HALO, D_MODEL)),
            vec, vec, vec,
            _resident((D_MODEL, D_MODEL)),
            vec, vec, vec,
        ],
        out_specs=pl.BlockSpec((1, CONV_SEQ_TILE, D_MODEL), lambda bb, tt: (bb, tt, 0)),
        scratch_shapes=[
            pltpu.VMEM((D_MODEL // LANES, CONV_HALO + CONV_SEQ_TILE, LANES), F32),
            pltpu.VMEM((SUBLANES - 1, D_MODEL // LANES, CONV_HALO - SUBLANES + CONV_SEQ_TILE, LANES), F32),
            pltpu.VMEM((D_MODEL // LANES, CONV_SEQ_TILE, LANES), F32),
        ],
        compiler_params=_params("parallel", "arbitrary"),
        name="conformer_conv",
    )(x3d, w1, b1, dw_w, dw_b, cg, cb, w2, b2, g, b)


def _row(v):
    return v.reshape(1, -1).astype(F32)


def _pad_lanes(v):
    return jnp.pad(v.reshape(1, -1).astype(F32), ((0, 0), (0, LANES - v.shape[-1])))


def kernel(x, mix_w_in, ssd_conv_w, ssd_conv_b, ssd_dt_bias, ssd_a_log, ssd_d, ssd_norm_w, mix_w_out, conv_w_pw1, conv_b_pw1, conv_dw_w, conv_dw_b, conv_ln_g, conv_ln_b, conv_w_pw2, conv_b_pw2, ffn_w_gate, ffn_w_up, ffn_w_down, ln_mix_g, ln_mix_b, ln_ffn_g, ln_ffn_b):
    bsz, seq, d = x.shape
    t = bsz * seq
    n_heads = ATTN_WIDTH // HEAD_DIM
    slopes = 2.0 ** (-8.0 * jnp.arange(1, n_heads + 1, dtype=F32) / n_heads)

    w_in = jnp.pad(mix_w_in[0], ((0, 0), (0, IN_COLS_PAD - mix_w_in.shape[-1]))).astype(BF16)
    h = _inproj(x.reshape(t, d), w_in).reshape(bsz, seq, IN_COLS_PAD)
    attn = _moba(h, slopes)
    y = _ssd(h, ssd_conv_w[0], _row(ssd_conv_b[0]), _pad_lanes(ssd_dt_bias[0]), _pad_lanes(ssd_a_log[0]),
             _row(jnp.repeat(ssd_d[0], HEAD_DIM)), _row(ssd_norm_w[0]))
    x2 = _mix_ffn(attn.reshape(t, ATTN_WIDTH), y.reshape(t, SSD_INNER), x.reshape(t, d),
                  mix_w_out[0].astype(BF16), _row(ln_mix_g[0]), _row(ln_mix_b[0]),
                  ffn_w_gate[0].astype(BF16), ffn_w_up[0].astype(BF16), ffn_w_down[0].astype(BF16),
                  _row(ln_ffn_g[0]), _row(ln_ffn_b[0]))

    dw_w = jnp.pad(conv_dw_w[0], ((0, CONV_HALO - CONF_KERNEL), (0, 0)))
    x3 = _conformer(x2.reshape(bsz, seq, d), conv_w_pw1[0].astype(BF16), _row(conv_b_pw1[0]),
                    dw_w, _row(conv_dw_b[0]), _row(conv_ln_g[0]), _row(conv_ln_b[0]),
                    conv_w_pw2[0].astype(BF16), _row(conv_b_pw2[0]),
                    _row(ln_mix_g[1]), _row(ln_mix_b[1]))
    x4 = _ffn_ln(x3.reshape(t, d), ffn_w_gate[1].astype(BF16), ffn_w_up[1].astype(BF16),
                 ffn_w_down[1].astype(BF16), _row(ln_ffn_g[1]), _row(ln_ffn_b[1]))
    return x4.reshape(bsz, seq, d)
```

```python
import functools

import jax
import jax.numpy as jnp
from jax import lax
from jax.experimental import pallas as pl
from jax.experimental.pallas import tpu as pltpu

F32 = jnp.float32
BF16 = jnp.bfloat16

D_MODEL = 1024
DEPTH = 2
ATTN_WIDTH = 512
HEAD_DIM = 64
MOBA_BLOCK = 256
MOBA_TOPK = 3
SSD_INNER = 512
SSD_HEADS = 8
SSD_GROUPS = 2
SSD_STATE = 128
SSD_CONV = 4
SSD_CHUNK = 256
SSD_CONV_CH = SSD_INNER + 2 * SSD_GROUPS * SSD_STATE
CONF_KERNEL = 31
DN_ALPHA = (2 * DEPTH) ** 0.25
LN_EPS = 1e-5
RMS_EPS = 1e-5

LANES = 128
SUBLANES = 8

COL_Q = 0
COL_K = ATTN_WIDTH
COL_V = 2 * ATTN_WIDTH
COL_Z = 3 * ATTN_WIDTH
COL_XBC = COL_Z + SSD_INNER
COL_DT = COL_XBC + SSD_CONV_CH
IN_COLS_PAD = COL_DT + LANES

MASK_NEG = -1e30
LOG2_E = 1.4426950408889634

ROW_TILE = 512
INPROJ_COL_CHUNK = 512
CONV_SEQ_TILE = 512
CONV_HALO = 32
CONV_ROW_CHUNK = 128
FFN_HID_PIECE = 1024
FFN_SLABS_PER_PIECE = 3
SSD_HALO = 8
VMEM_LIMIT = 56 * 1024 * 1024


def _params(*sem):
    return pltpu.CompilerParams(dimension_semantics=sem, vmem_limit_bytes=VMEM_LIMIT)


def _dot(a, b):
    return jnp.dot(a, b, preferred_element_type=F32)


def _dot_nt(a, b):
    return lax.dot_general(a, b, (((1,), (1,)), ((), ())), preferred_element_type=F32)


def _split3(x):
    hi = x.astype(BF16)
    r1 = x - hi.astype(F32)
    mid = r1.astype(BF16)
    lo = (r1 - mid.astype(F32)).astype(BF16)
    return hi, mid, lo


def _silu(x):
    return x * jax.nn.sigmoid(x)


def _layer_norm(r, g, b):
    mu = jnp.mean(r, axis=-1, keepdims=True)
    d = r - mu
    var = jnp.mean(d * d, axis=-1, keepdims=True)
    return d * lax.rsqrt(var + LN_EPS) * g + b


def _inproj_kernel(x_ref, w_ref, o_ref):
    xb = x_ref[...].astype(BF16)
    for c0 in range(0, IN_COLS_PAD, INPROJ_COL_CHUNK):
        c1 = min(c0 + INPROJ_COL_CHUNK, IN_COLS_PAD)
        o_ref[:, c0:c1] = _dot(xb, w_ref[:, c0:c1])


def _resident(shape):
    return pl.BlockSpec(shape, lambda *_: (0,) * len(shape), pipeline_mode=pl.Buffered(1))


def _inproj(x2d, w_pad):
    t = x2d.shape[0]
    return pl.pallas_call(
        _inproj_kernel,
        out_shape=jax.ShapeDtypeStruct((t, IN_COLS_PAD), F32),
        grid=(t // ROW_TILE,),
        in_specs=[
            pl.BlockSpec((ROW_TILE, D_MODEL), lambda i: (i, 0)),
            _resident((D_MODEL, IN_COLS_PAD)),
        ],
        out_specs=pl.BlockSpec((ROW_TILE, IN_COLS_PAD), lambda i: (i, 0)),
        compiler_params=_params("parallel"),
        name="inproj",
    )(x2d, w_pad)


def _moba_kernel(slopes_ref, q_ref, k_ref, v_ref, o_ref, kb_ref, vt_ref, km_ref, al_ref, s_ref):
    hp = pl.program_id(1)
    nb = kb_ref.shape[0]
    blk = MOBA_BLOCK
    scale = HEAD_DIM ** -0.5 * LOG2_E

    for n in range(nb):
        kblk = k_ref[0, n * blk:(n + 1) * blk, :]
        kb_ref[n] = kblk.astype(BF16)
        km_ref[n:n + 1, :] = jnp.sum(kblk, axis=0, keepdims=True) * (1.0 / blk)
        vt_ref[n] = v_ref[0, n * blk:(n + 1) * blk, :].T.astype(BF16)

    lane = lax.broadcasted_iota(jnp.int32, (blk, LANES), 1)
    key_in = lax.broadcasted_iota(jnp.int32, (blk, blk), 0)
    qry_in = lax.broadcasted_iota(jnp.int32, (blk, blk), 1)
    dist0 = (qry_in - key_in).astype(F32)
    causal = qry_in >= key_in
    km = km_ref[...]
    km_hi = km.astype(BF16)
    km_lo = (km - km_hi.astype(F32)).astype(BF16)
    vt_rows = [slice(hd * HEAD_DIM, (hd + 1) * HEAD_DIM) for hd in range(2)]
    head_lanes = [(lane >= hd * HEAD_DIM) & (lane < (hd + 1) * HEAD_DIM) for hd in range(2)]
    slopes = [slopes_ref[2 * hp + hd] * LOG2_E for hd in range(2)]
    for hd in range(2):
        al = -slopes[hd] * dist0
        al_ref[hd, 0] = al
        al_ref[hd, 1] = jnp.where(causal, al, MASK_NEG)

    for i in range(nb):
        q = q_ref[0, i * blk:(i + 1) * blk, :]
        outs = []
        for hd in range(2):
            slope = slopes[hd]
            q_f = jnp.where(head_lanes[hd], q, 0.0) * scale
            q_hi = q_f.astype(BF16)
            if i > MOBA_TOPK:
                q_lo = (q_f - q_hi.astype(F32)).astype(BF16)
                gate = (_dot_nt(km_hi, q_hi) + _dot_nt(km_lo, q_hi)
                        + _dot_nt(km_hi, q_lo))
                blk_id = lax.broadcasted_iota(jnp.int32, gate.shape, 0)
                rank = jnp.zeros(gate.shape, jnp.int32)
                for m in range(i):
                    gm = gate[m:m + 1, :]
                    beats = (gm > gate) | ((gm == gate) & (blk_id > m))
                    rank = rank + jnp.where(beats & (blk_id != m), 1, 0)
                blk_dist = ((i - blk_id) * blk).astype(F32)
                rb = jnp.where(rank < MOBA_TOPK, -slope * blk_dist, MASK_NEG)
                row_bias = [rb[j:j + 1, :] for j in range(i)]
            else:
                row_bias = [-slope * float((i - j) * blk) for j in range(i)]

            sbuf = s_ref.at[i % 2, hd]
            s = _dot_nt(kb_ref[i], q_hi) + al_ref[hd, 1]
            sbuf[i] = s
            m_max = jnp.max(s, axis=0, keepdims=True)
            for j in range(i):
                s = _dot_nt(kb_ref[j], q_hi) + al_ref[hd, 0] + row_bias[j]
                sbuf[j] = s
                m_max = jnp.maximum(m_max, jnp.max(s, axis=0, keepdims=True))
            l_sum = jnp.zeros((1, blk), F32)
            acc = jnp.zeros((HEAD_DIM, blk), F32)
            for j in range(i + 1):
                p = jnp.exp2(sbuf[j] - m_max)
                l_sum = l_sum + jnp.sum(p, axis=0, keepdims=True)
                acc = acc + _dot(vt_ref[j, vt_rows[hd], :], p.astype(BF16))
            outs.append(acc / l_sum)
        o_ref[0, i * blk:(i + 1) * blk, :] = jnp.concatenate(outs, axis=0).T


def _moba(h3d, slopes):
    bsz, seq, _ = h3d.shape
    nb = seq // MOBA_BLOCK
    n_pairs = ATTN_WIDTH // LANES
    return pl.pallas_call(
        _moba_kernel,
        out_shape=jax.ShapeDtypeStruct((bsz, seq, ATTN_WIDTH), F32),
        grid=(bsz, n_pairs),
        in_specs=[
            pl.BlockSpec(memory_space=pltpu.SMEM),
            pl.BlockSpec((1, seq, LANES), lambda b, p: (b, 0, COL_Q // LANES + p)),
            pl.BlockSpec((1, seq, LANES), lambda b, p: (b, 0, COL_K // LANES + p)),
            pl.BlockSpec((1, seq, LANES), lambda b, p: (b, 0, COL_V // LANES + p)),
        ],
        out_specs=pl.BlockSpec((1, seq, LANES), lambda b, p: (b, 0, p)),
        scratch_shapes=[
            pltpu.VMEM((nb, MOBA_BLOCK, LANES), BF16),
            pltpu.VMEM((nb, LANES, MOBA_BLOCK), BF16),
            pltpu.VMEM((nb, LANES), F32),
            pltpu.VMEM((2, 2, MOBA_BLOCK, MOBA_BLOCK), F32),
            pltpu.VMEM((2, 2, nb, MOBA_BLOCK, MOBA_BLOCK), F32),
        ],
        compiler_params=_params("parallel", "parallel"),
        name="moba_attention",
    )(slopes, h3d, h3d, h3d)


def _ssd_kernel(xbc_ref, dt_ref, z_ref, cw_ref, cb_ref, dtb_ref, alog_ref, dskip_ref, nw_ref,
                y_ref, xpad_ref, hst_ref):
    c = pl.program_id(1)
    L = SSD_CHUNK
    gw = SSD_INNER // SSD_GROUPS

    @pl.when(c == 0)
    def _():
        xpad_ref[0:SSD_HALO, :] = jnp.zeros((SSD_HALO, SSD_CONV_CH), F32)
        hst_ref[...] = jnp.zeros(hst_ref.shape, F32)

    xpad_ref[SSD_HALO:SSD_HALO + L, :] = xbc_ref[0]
    conv = jnp.broadcast_to(cb_ref[...], (L, SSD_CONV_CH))
    for k in range(SSD_CONV):
        off = SSD_HALO - (SSD_CONV - 1) + k
        conv = conv + cw_ref[k:k + 1, :] * xpad_ref[off:off + L, :]
    xpad_ref[0:SSD_HALO, :] = xpad_ref[L:L + SSD_HALO, :]
    xbc = _silu(conv)
    xs = xbc[:, :SSD_INNER]

    dt_in = dt_ref[0] + dtb_ref[...]
    dt = jnp.maximum(dt_in, 0.0) + jnp.log1p(jnp.exp(-jnp.abs(dt_in)))
    a = -jnp.exp(alog_ref[...])
    a_dt = dt * a
    row = lax.broadcasted_iota(jnp.int32, (L, L), 0)
    col = lax.broadcasted_iota(jnp.int32, (L, L), 1)
    causal = row >= col
    tril = jnp.where(causal, 1.0, 0.0).astype(BF16)
    a_cs = _dot_exact_rhs_lhs(tril, a_dt)
    a_cs_t = a_cs.T
    a_last = a_cs[L - 1:L, :]
    decay_to_end = jnp.exp(a_last - a_cs)
    decay_from_start = jnp.exp(a_cs)

    eh = lax.broadcasted_iota(jnp.int32, (2 * LANES, SSD_INNER), 0) & (LANES - 1)
    ec = lax.broadcasted_iota(jnp.int32, (2 * LANES, SSD_INNER), 1)
    expand = jnp.where((ec >= eh * HEAD_DIM) & (ec < (eh + 1) * HEAD_DIM), 1.0, 0.0).astype(BF16)
    per_head = jnp.concatenate([dt, decay_to_end, decay_from_start], axis=0)
    ph_hi = per_head.astype(BF16)
    ph_lo = (per_head - ph_hi.astype(F32)).astype(BF16)
    per_chan = _dot(jnp.concatenate([ph_hi, ph_lo], axis=1), expand)
    dt_full = per_chan[0:L]
    dte_full = per_chan[L:2 * L]
    dfs_full = per_chan[2 * L:3 * L]

    x_dt = xs * dt_full
    x_dte = (x_dt * dte_full).astype(BF16)
    x_dt_b = x_dt.astype(BF16)
    first_head = lax.broadcasted_iota(jnp.int32, (L, LANES), 1) < HEAD_DIM
    heads_per_group = SSD_HEADS // SSD_GROUPS

    for g in range(SSD_GROUPS):
        bm = xbc[:, SSD_INNER + g * SSD_STATE:SSD_INNER + (g + 1) * SSD_STATE]
        cm = xbc[:, SSD_INNER + (SSD_GROUPS + g) * SSD_STATE:SSD_INNER + (SSD_GROUPS + g + 1) * SSD_STATE]
        bm_b = bm.astype(BF16)
        cm_b = cm.astype(BF16)
        cbm = _dot_nt(cm_b, bm_b)
        gs = slice(g * gw, (g + 1) * gw)

        h_prev = hst_ref[g]
        y_g = _dot(cm_b, h_prev.astype(BF16)) * dfs_full[:, gs]
        st = _dot(bm.T.astype(BF16), x_dte[:, gs])
        hst_ref[g] = h_prev * dfs_full[L - 1:L, gs] + st

        pieces = []
        for pi in range(heads_per_group // 2):
            ls = slice(g * gw + pi * LANES, g * gw + (pi + 1) * LANES)
            xp = x_dt_b[:, ls]
            yd = []
            for hd in range(2):
                h = g * heads_per_group + 2 * pi + hd
                seg = a_cs[:, h:h + 1] - a_cs_t[h:h + 1, :]
                dec = jnp.exp(jnp.where(causal, seg, MASK_NEG))
                yd.append(_dot((cbm * dec).astype(BF16), xp))
            pieces.append(jnp.where(first_head, yd[0], yd[1]))
        y_g = y_g + jnp.concatenate(pieces, axis=1) + xs[:, gs] * dskip_ref[:, gs]

        gz = y_g * _silu(z_ref[0, :, gs])
        ms = jnp.mean(gz * gz, axis=-1, keepdims=True)
        y_ref[0, :, gs] = gz * lax.rsqrt(ms + RMS_EPS) * nw_ref[:, gs]


def _dot_exact_rhs_lhs(lhs_bf16, x):
    hi, mid, lo = _split3(x)
    return _dot(lhs_bf16, hi) + _dot(lhs_bf16, mid) + _dot(lhs_bf16, lo)


def _ssd(h3d, conv_w, conv_b, dt_bias, a_log, d_full, norm_w):
    bsz, seq, _ = h3d.shape
    n_c = seq // SSD_CHUNK
    const = lambda b, c: (0, 0)
    return pl.pallas_call(
        _ssd_kernel,
        out_shape=jax.ShapeDtypeStruct((bsz, seq, SSD_INNER), F32),
        grid=(bsz, n_c),
        in_specs=[
            pl.BlockSpec((1, SSD_CHUNK, SSD_CONV_CH), lambda b, c: (b, c, COL_XBC // SSD_CONV_CH)),
            pl.BlockSpec((1, SSD_CHUNK, LANES), lambda b, c: (b, c, COL_DT // LANES)),
            pl.BlockSpec((1, SSD_CHUNK, SSD_INNER), lambda b, c: (b, c, COL_Z // SSD_INNER)),
            pl.BlockSpec((SSD_CONV, SSD_CONV_CH), const),
            pl.BlockSpec((1, SSD_CONV_CH), const),
            pl.BlockSpec((1, LANES), const),
            pl.BlockSpec((1, LANES), const),
            pl.BlockSpec((1, SSD_INNER), const),
            pl.BlockSpec((1, SSD_INNER), const),
        ],
        out_specs=pl.BlockSpec((1, SSD_CHUNK, SSD_INNER), lambda b, c: (b, c, 0)),
        scratch_shapes=[
            pltpu.VMEM((SSD_HALO + SSD_CHUNK, SSD_CONV_CH), F32),
            pltpu.VMEM((SSD_GROUPS, SSD_STATE, SSD_INNER // SSD_GROUPS), F32),
        ],
        compiler_params=_params("parallel", "arbitrary"),
        name="ssd_scan",
    )(h3d, h3d, h3d, conv_w, conv_b, dt_bias, a_log, d_full, norm_w)


def _ffn_ln_value(x, wg_ref, wu_ref, wd_ref, g_ref, b_ref):
    xb = x.astype(BF16)
    gate = _dot(xb, wg_ref[...])
    up = _dot(xb, wu_ref[...])
    ffn = _dot((_silu(gate) * up).astype(BF16), wd_ref[...])
    return _layer_norm(DN_ALPHA * x + ffn, g_ref[...], b_ref[...])


def _ffn_specs(hid, layer):
    def stacked(rows, cols):
        return pl.BlockSpec((None, rows, cols), lambda *_: (layer, 0, 0), pipeline_mode=pl.Buffered(1))
    return [stacked(D_MODEL, hid), stacked(D_MODEL, hid), stacked(hid, D_MODEL),
            _resident((1, D_MODEL)), _resident((1, D_MODEL))]


def _mix_ffn_kernel(attn_ref, y_ref, x_ref, wo_ref, g1_ref, b1_ref,
                    wg_ref, wu_ref, wd_ref, g2_ref, b2_ref, o_ref):
    mixed = (_dot(attn_ref[...].astype(BF16), wo_ref[0:ATTN_WIDTH, :])
             + _dot(y_ref[...].astype(BF16), wo_ref[ATTN_WIDTH:, :]))
    x1 = _layer_norm(DN_ALPHA * x_ref[...] + mixed, g1_ref[...], b1_ref[...])
    o_ref[...] = _ffn_ln_value(x1, wg_ref, wu_ref, wd_ref, g2_ref, b2_ref)


def _mix_ffn(attn2d, y2d, x2d, w_out, g1, b1, layer, w_gate, w_up, w_down, g2, b2):
    t = x2d.shape[0]
    return pl.pallas_call(
        _mix_ffn_kernel,
        out_shape=jax.ShapeDtypeStruct((t, D_MODEL), F32),
        grid=(t // ROW_TILE,),
        in_specs=[
            pl.BlockSpec((ROW_TILE, ATTN_WIDTH), lambda i: (i, 0)),
            pl.BlockSpec((ROW_TILE, SSD_INNER), lambda i: (i, 0)),
            pl.BlockSpec((ROW_TILE, D_MODEL), lambda i: (i, 0)),
            _resident((ATTN_WIDTH + SSD_INNER, D_MODEL)),
            _resident((1, D_MODEL)),
            _resident((1, D_MODEL)),
        ] + _ffn_specs(w_gate.shape[-1], layer),
        out_specs=pl.BlockSpec((ROW_TILE, D_MODEL), lambda i: (i, 0)),
        compiler_params=_params("parallel"),
        name="outproj_ffn_ln",
    )(attn2d, y2d, x2d, w_out, g1, b1, w_gate, w_up, w_down, g2, b2)


def _conformer_ffn_kernel(x_ref, w1_ref, b1_ref, dww_ref, dwb_ref, cg_ref, cb_ref, w2_ref, b2_ref,
                          g_ref, b_ref, wg_ref, wu_ref, wd_ref, g2_ref, b2f_ref, o_ref,
                          hpad_ref, shift_ref, cv_ref, x3_ref, *, tiles_per_seq):
    t = pl.program_id(0)
    ts = CONV_SEQ_TILE
    lead = CONV_HALO - SUBLANES
    n_slabs = D_MODEL // LANES

    @pl.when(t == 0)
    def _():
        x3_ref[...] = jnp.zeros(x3_ref.shape, F32)

    @pl.when(t % tiles_per_seq == 0)
    def _():
        hpad_ref[:, 0:CONV_HALO, :] = jnp.zeros((n_slabs, CONV_HALO, LANES), F32)

    xin = x3_ref[(t + 1) % 2]
    xin_b = xin.astype(BF16)

    x = x_ref[...]
    xb = x.astype(BF16)
    val = _dot(xb, w1_ref[:, :D_MODEL]) + b1_ref[:, :D_MODEL]
    gate = _dot(xb, w1_ref[:, D_MODEL:]) + b1_ref[:, D_MODEL:]
    h = val * jax.nn.sigmoid(gate)
    for cc in range(n_slabs):
        hpad_ref[cc, CONV_HALO:CONV_HALO + ts, :] = h[:, cc * LANES:(cc + 1) * LANES]

    tiles = CONV_ROW_CHUNK // SUBLANES
    hid = wg_ref.shape[-1]
    ffn = None
    for cc in range(n_slabs):
        if cc % FFN_SLABS_PER_PIECE == 0:
            c0 = cc // FFN_SLABS_PER_PIECE * FFN_HID_PIECE
            c1 = min(c0 + FFN_HID_PIECE, hid)
            act = _silu(_dot(xin_b, wg_ref[:, c0:c1])) * _dot(xin_b, wu_ref[:, c0:c1])
            part = _dot(act.astype(BF16), wd_ref[c0:c1, :])
            ffn = part if ffn is None else ffn + part
        cs = slice(cc * LANES, (cc + 1) * LANES)
        shifted = shift_ref.at[cc % 2]
        for r in range(1, SUBLANES):
            shifted[r - 1] = hpad_ref[cc, SUBLANES - r:SUBLANES - r + lead + ts, :]
        taps = [jnp.broadcast_to(dww_ref[k:k + 1, cs], (SUBLANES, LANES)) for k in range(CONF_KERNEL)]
        bias = jnp.broadcast_to(dwb_ref[:, cs], (SUBLANES, LANES))
        for r0 in range(0, ts, CONV_ROW_CHUNK):
            acc = jnp.broadcast_to(bias[None], (tiles, SUBLANES, LANES))
            for k in range(CONF_KERNEL):
                a, r = divmod(CONF_KERNEL - 1 - k, SUBLANES)
                if r == 0:
                    lo = r0 + CONV_HALO - SUBLANES * a
                    src = hpad_ref[cc, lo:lo + CONV_ROW_CHUNK, :]
                else:
                    lo = r0 + lead - SUBLANES * a
                    src = shifted[r - 1, lo:lo + CONV_ROW_CHUNK, :]
                acc = acc + taps[k][None] * src.reshape(tiles, SUBLANES, LANES)
            cv_ref[cc, r0:r0 + CONV_ROW_CHUNK, :] = acc.reshape(CONV_ROW_CHUNK, LANES)
    hpad_ref[:, 0:CONV_HALO, :] = hpad_ref[:, ts:ts + CONV_HALO, :]
    o_ref[...] = _layer_norm(DN_ALPHA * xin + ffn, g2_ref[...], b2f_ref[...])

    conv = jnp.concatenate([cv_ref[cc] for cc in range(n_slabs)], axis=1)
    hn = _silu(_layer_norm(conv, cg_ref[...], cb_ref[...]))
    mixed = _dot(hn.astype(BF16), w2_ref[...]) + b2_ref[...]
    x3_ref[t % 2] = _layer_norm(DN_ALPHA * x + mixed, g_ref[...], b_ref[...])


def _conformer_ffn(x2d, seq, w1, b1, dw_w, dw_b, cg, cb, w2, b2, g, b, layer, w_gate, w_up, w_down,
                   g2, b2f):
    t = x2d.shape[0]
    n_tiles = t // CONV_SEQ_TILE
    vec = _resident((1, D_MODEL))
    rows = (CONV_SEQ_TILE, D_MODEL)
    return pl.pallas_call(
        functools.partial(_conformer_ffn_kernel, tiles_per_seq=seq // CONV_SEQ_TILE),
        out_shape=jax.ShapeDtypeStruct((t, D_MODEL), F32),
        grid=(n_tiles + 1,),
        in_specs=[
            pl.BlockSpec(rows, lambda i: (jnp.minimum(i, n_tiles - 1), 0)),
            _resident((D_MODEL, 2 * D_MODEL)),
            _resident((1, 2 * D_MODEL)),
            _resident((CONV_HALO, D_MODEL)),
            vec, vec, vec,
            _resident((D_MODEL, D_MODEL)),
            vec, vec, vec,
        ] + _ffn_specs(w_gate.shape[-1], layer),
        out_specs=pl.BlockSpec(rows, lambda i: (jnp.maximum(i - 1, 0), 0)),
        scratch_shapes=[
            pltpu.VMEM((D_MODEL // LANES, CONV_HALO + CONV_SEQ_TILE, LANES), F32),
            pltpu.VMEM((2, SUBLANES - 1, CONV_HALO - SUBLANES + CONV_SEQ_TILE, LANES), F32),
            pltpu.VMEM((D_MODEL // LANES, CONV_SEQ_TILE, LANES), F32),
            pltpu.VMEM((2,) + rows, F32),
        ],
        compiler_params=_params("arbitrary"),
        name="conformer_ffn_ln",
    )(x2d, w1, b1, dw_w, dw_b, cg, cb, w2, b2, g, b, w_gate, w_up, w_down, g2, b2f)


def _row(v):
    return v.reshape(1, -1).astype(F32)


def _pad_lanes(v):
    return jnp.pad(v.reshape(1, -1).astype(F32), ((0, 0), (0, LANES - v.shape[-1])))


def kernel(x, mix_w_in, ssd_conv_w, ssd_conv_b, ssd_dt_bias, ssd_a_log, ssd_d, ssd_norm_w, mix_w_out, conv_w_pw1, conv_b_pw1, conv_dw_w, conv_dw_b, conv_ln_g, conv_ln_b, conv_w_pw2, conv_b_pw2, ffn_w_gate, ffn_w_up, ffn_w_down, ln_mix_g, ln_mix_b, ln_ffn_g, ln_ffn_b):
    bsz, seq, d = x.shape
    t = bsz * seq
    n_heads = ATTN_WIDTH // HEAD_DIM
    slopes = 2.0 ** (-8.0 * jnp.arange(1, n_heads + 1, dtype=F32) / n_heads)

    w_in = jnp.pad(mix_w_in[0], ((0, 0), (0, IN_COLS_PAD - mix_w_in.shape[-1]))).astype(BF16)
    h = _inproj(x.reshape(t, d), w_in).reshape(bsz, seq, IN_COLS_PAD)
    attn = _moba(h, slopes)
    y = _ssd(h, ssd_conv_w[0], _row(ssd_conv_b[0]), _pad_lanes(ssd_dt_bias[0]), _pad_lanes(ssd_a_log[0]),
             _row(jnp.repeat(ssd_d[0], HEAD_DIM)), _row(ssd_norm_w[0]))
    ffn_w = (ffn_w_gate.astype(BF16), ffn_w_up.astype(BF16), ffn_w_down.astype(BF16))
    x2 = _mix_ffn(attn.reshape(t, ATTN_WIDTH), y.reshape(t, SSD_INNER), x.reshape(t, d),
                  mix_w_out[0].astype(BF16), _row(ln_mix_g[0]), _row(ln_mix_b[0]),
                  0, *ffn_w, _row(ln_ffn_g[0]), _row(ln_ffn_b[0]))

    dw_w = jnp.pad(conv_dw_w[0], ((0, CONV_HALO - CONF_KERNEL), (0, 0)))
    x4 = _conformer_ffn(x2, seq, conv_w_pw1[0].astype(BF16), _row(conv_b_pw1[0]),
                        dw_w, _row(conv_dw_b[0]), _row(conv_ln_g[0]), _row(conv_ln_b[0]),
                        conv_w_pw2[0].astype(BF16), _row(conv_b_pw2[0]),
                        _row(ln_mix_g[1]), _row(ln_mix_b[1]),
                        1, *ffn_w, _row(ln_ffn_g[1]), _row(ln_ffn_b[1]))
    return x4.reshape(bsz, seq, d)
```

```python
import functools

import jax
import jax.numpy as jnp
from jax import lax
from jax.experimental import pallas as pl
from jax.experimental.pallas import tpu as pltpu

F32 = jnp.float32
BF16 = jnp.bfloat16

D_MODEL = 1024
DEPTH = 2
ATTN_WIDTH = 512
HEAD_DIM = 64
MOBA_BLOCK = 256
MOBA_TOPK = 3
SSD_INNER = 512
SSD_HEADS = 8
SSD_GROUPS = 2
SSD_STATE = 128
SSD_CONV = 4
SSD_CHUNK = 256
SSD_CONV_CH = SSD_INNER + 2 * SSD_GROUPS * SSD_STATE
CONF_KERNEL = 31
DN_ALPHA = (2 * DEPTH) ** 0.25
LN_EPS = 1e-5
RMS_EPS = 1e-5

LANES = 128
SUBLANES = 8
BF16_ROWS = 16

COL_Q = 0
COL_K = ATTN_WIDTH
COL_V = 2 * ATTN_WIDTH
COL_Z = 3 * ATTN_WIDTH
COL_XBC = COL_Z + SSD_INNER
COL_DT = COL_XBC + SSD_CONV_CH
IN_COLS_PAD = COL_DT + LANES

MASK_NEG = -1e30
LOG2_E = 1.4426950408889634

ROW_TILE = 512
INPROJ_COL_CHUNK = 512
CONV_SEQ_TILE = 512
CONV_HALO = 32
CONV_ROW_CHUNK = 64
FFN_HID_PIECE = 1024
FFN_SLABS_PER_PIECE = 3
SSD_HALO = 8
VMEM_LIMIT = 56 * 1024 * 1024


def _params(*sem):
    return pltpu.CompilerParams(dimension_semantics=sem, vmem_limit_bytes=VMEM_LIMIT)


def _dot(a, b):
    return jnp.dot(a, b, preferred_element_type=F32)


def _dot_nt(a, b):
    return lax.dot_general(a, b, (((1,), (1,)), ((), ())), preferred_element_type=F32)


def _split3(x):
    hi = x.astype(BF16)
    r1 = x - hi.astype(F32)
    mid = r1.astype(BF16)
    lo = (r1 - mid.astype(F32)).astype(BF16)
    return hi, mid, lo


def _silu(x):
    return x * jax.nn.sigmoid(x)


def _layer_norm(r, g, b):
    mu = jnp.mean(r, axis=-1, keepdims=True)
    d = r - mu
    var = jnp.mean(d * d, axis=-1, keepdims=True)
    return d * lax.rsqrt(var + LN_EPS) * g + b


def _inproj_kernel(x_ref, w_ref, o_ref):
    xb = x_ref[...].astype(BF16)
    for c0 in range(0, IN_COLS_PAD, INPROJ_COL_CHUNK):
        c1 = min(c0 + INPROJ_COL_CHUNK, IN_COLS_PAD)
        o_ref[:, c0:c1] = _dot(xb, w_ref[:, c0:c1])


def _resident(shape):
    return pl.BlockSpec(shape, lambda *_: (0,) * len(shape), pipeline_mode=pl.Buffered(1))


def _inproj(x2d, w_pad):
    t = x2d.shape[0]
    return pl.pallas_call(
        _inproj_kernel,
        out_shape=jax.ShapeDtypeStruct((t, IN_COLS_PAD), F32),
        grid=(t // ROW_TILE,),
        in_specs=[
            pl.BlockSpec((ROW_TILE, D_MODEL), lambda i: (i, 0)),
            _resident((D_MODEL, IN_COLS_PAD)),
        ],
        out_specs=pl.BlockSpec((ROW_TILE, IN_COLS_PAD), lambda i: (i, 0)),
        compiler_params=_params("parallel"),
        name="inproj",
    )(x2d, w_pad)


def _moba_kernel(slopes_ref, q_ref, k_ref, v_ref, o_ref, kb_ref, vt_ref, km_ref, al_ref, s_ref):
    hp = pl.program_id(1)
    nb = kb_ref.shape[0]
    blk = MOBA_BLOCK
    scale = HEAD_DIM ** -0.5 * LOG2_E

    for n in range(nb):
        kblk = k_ref[0, n * blk:(n + 1) * blk, :]
        kb_ref[n] = kblk.astype(BF16)
        km_ref[n:n + 1, :] = jnp.sum(kblk, axis=0, keepdims=True) * (1.0 / blk)
        vt_ref[n] = v_ref[0, n * blk:(n + 1) * blk, :].T.astype(BF16)

    lane = lax.broadcasted_iota(jnp.int32, (blk, LANES), 1)
    key_in = lax.broadcasted_iota(jnp.int32, (blk, blk), 0)
    qry_in = lax.broadcasted_iota(jnp.int32, (blk, blk), 1)
    dist0 = (qry_in - key_in).astype(F32)
    causal = qry_in >= key_in
    km = km_ref[...]
    km_hi = km.astype(BF16)
    km_lo = (km - km_hi.astype(F32)).astype(BF16)
    vt_rows = [slice(hd * HEAD_DIM, (hd + 1) * HEAD_DIM) for hd in range(2)]
    head_lanes = [(lane >= hd * HEAD_DIM) & (lane < (hd + 1) * HEAD_DIM) for hd in range(2)]
    slopes = [slopes_ref[2 * hp + hd] * LOG2_E for hd in range(2)]
    for hd in range(2):
        al = -slopes[hd] * dist0
        al_ref[hd, 0] = al
        al_ref[hd, 1] = jnp.where(causal, al, MASK_NEG)

    for i in range(nb):
        q = q_ref[0, i * blk:(i + 1) * blk, :]
        outs = []
        for hd in range(2):
            slope = slopes[hd]
            q_f = jnp.where(head_lanes[hd], q, 0.0) * scale
            q_hi = q_f.astype(BF16)
            if i > MOBA_TOPK:
                q_lo = (q_f - q_hi.astype(F32)).astype(BF16)
                gate = (_dot_nt(km_hi, q_hi) + _dot_nt(km_lo, q_hi)
                        + _dot_nt(km_hi, q_lo))
                blk_id = lax.broadcasted_iota(jnp.int32, gate.shape, 0)
                rank = jnp.zeros(gate.shape, jnp.int32)
                for m in range(i):
                    gm = gate[m:m + 1, :]
                    beats = (gm > gate) | ((gm == gate) & (blk_id > m))
                    rank = rank + jnp.where(beats & (blk_id != m), 1, 0)
                blk_dist = ((i - blk_id) * blk).astype(F32)
                rb = jnp.where(rank < MOBA_TOPK, -slope * blk_dist, MASK_NEG)
                row_bias = [rb[j:j + 1, :] for j in range(i)]
            else:
                row_bias = [-slope * float((i - j) * blk) for j in range(i)]

            sbuf = s_ref.at[i % 2, hd]
            s = _dot_nt(kb_ref[i], q_hi) + al_ref[hd, 1]
            sbuf[i] = s
            m_max = jnp.max(s, axis=0, keepdims=True)
            for j in range(i):
                s = _dot_nt(kb_ref[j], q_hi) + al_ref[hd, 0] + row_bias[j]
                sbuf[j] = s
                m_max = jnp.maximum(m_max, jnp.max(s, axis=0, keepdims=True))
            l_sum = jnp.zeros((1, blk), F32)
            acc = jnp.zeros((HEAD_DIM, blk), F32)
            for j in range(i + 1):
                p = jnp.exp2(sbuf[j] - m_max)
                l_sum = l_sum + jnp.sum(p, axis=0, keepdims=True)
                acc = acc + _dot(vt_ref[j, vt_rows[hd], :], p.astype(BF16))
            outs.append(acc / l_sum)
        o_ref[0, i * blk:(i + 1) * blk, :] = jnp.concatenate(outs, axis=0).T


def _moba(h3d, slopes):
    bsz, seq, _ = h3d.shape
    nb = seq // MOBA_BLOCK
    n_pairs = ATTN_WIDTH // LANES
    return pl.pallas_call(
        _moba_kernel,
        out_shape=jax.ShapeDtypeStruct((bsz, seq, ATTN_WIDTH), F32),
        grid=(bsz, n_pairs),
        in_specs=[
            pl.BlockSpec(memory_space=pltpu.SMEM),
            pl.BlockSpec((1, seq, LANES), lambda b, p: (b, 0, COL_Q // LANES + p)),
            pl.BlockSpec((1, seq, LANES), lambda b, p: (b, 0, COL_K // LANES + p)),
            pl.BlockSpec((1, seq, LANES), lambda b, p: (b, 0, COL_V // LANES + p)),
        ],
        out_specs=pl.BlockSpec((1, seq, LANES), lambda b, p: (b, 0, p)),
        scratch_shapes=[
            pltpu.VMEM((nb, MOBA_BLOCK, LANES), BF16),
            pltpu.VMEM((nb, LANES, MOBA_BLOCK), BF16),
            pltpu.VMEM((nb, LANES), F32),
            pltpu.VMEM((2, 2, MOBA_BLOCK, MOBA_BLOCK), F32),
            pltpu.VMEM((2, 2, nb, MOBA_BLOCK, MOBA_BLOCK), F32),
        ],
        compiler_params=_params("parallel", "parallel"),
        name="moba_attention",
    )(slopes, h3d, h3d, h3d)


def _ssd_kernel(xbc_ref, dt_ref, z_ref, cw_ref, cb_ref, dtb_ref, alog_ref, dskip_ref, nw_ref,
                y_ref, xpad_ref, hst_ref):
    c = pl.program_id(1)
    L = SSD_CHUNK
    gw = SSD_INNER // SSD_GROUPS

    @pl.when(c == 0)
    def _():
        xpad_ref[0:SSD_HALO, :] = jnp.zeros((SSD_HALO, SSD_CONV_CH), F32)
        hst_ref[...] = jnp.zeros(hst_ref.shape, F32)

    xpad_ref[SSD_HALO:SSD_HALO + L, :] = xbc_ref[0]
    conv = jnp.broadcast_to(cb_ref[...], (L, SSD_CONV_CH))
    for k in range(SSD_CONV):
        off = SSD_HALO - (SSD_CONV - 1) + k
        conv = conv + cw_ref[k:k + 1, :] * xpad_ref[off:off + L, :]
    xpad_ref[0:SSD_HALO, :] = xpad_ref[L:L + SSD_HALO, :]
    xbc = _silu(conv)
    xs = xbc[:, :SSD_INNER]

    dt_in = dt_ref[0] + dtb_ref[...]
    dt = jnp.maximum(dt_in, 0.0) + jnp.log1p(jnp.exp(-jnp.abs(dt_in)))
    a = -jnp.exp(alog_ref[...])
    a_dt = dt * a
    row = lax.broadcasted_iota(jnp.int32, (L, L), 0)
    col = lax.broadcasted_iota(jnp.int32, (L, L), 1)
    causal = row >= col
    tril = jnp.where(causal, 1.0, 0.0).astype(BF16)
    a_cs = _dot_exact_rhs_lhs(tril, a_dt)
    a_cs_t = a_cs.T
    a_last = a_cs[L - 1:L, :]
    decay_to_end = jnp.exp(a_last - a_cs)
    decay_from_start = jnp.exp(a_cs)

    eh = lax.broadcasted_iota(jnp.int32, (2 * LANES, SSD_INNER), 0) & (LANES - 1)
    ec = lax.broadcasted_iota(jnp.int32, (2 * LANES, SSD_INNER), 1)
    expand = jnp.where((ec >= eh * HEAD_DIM) & (ec < (eh + 1) * HEAD_DIM), 1.0, 0.0).astype(BF16)
    per_head = jnp.concatenate([dt, decay_to_end, decay_from_start], axis=0)
    ph_hi = per_head.astype(BF16)
    ph_lo = (per_head - ph_hi.astype(F32)).astype(BF16)
    per_chan = _dot(jnp.concatenate([ph_hi, ph_lo], axis=1), expand)
    dt_full = per_chan[0:L]
    dte_full = per_chan[L:2 * L]
    dfs_full = per_chan[2 * L:3 * L]

    x_dt = xs * dt_full
    x_dte = (x_dt * dte_full).astype(BF16)
    x_dt_b = x_dt.astype(BF16)
    first_head = lax.broadcasted_iota(jnp.int32, (L, LANES), 1) < HEAD_DIM
    heads_per_group = SSD_HEADS // SSD_GROUPS

    for g in range(SSD_GROUPS):
        bm = xbc[:, SSD_INNER + g * SSD_STATE:SSD_INNER + (g + 1) * SSD_STATE]
        cm = xbc[:, SSD_INNER + (SSD_GROUPS + g) * SSD_STATE:SSD_INNER + (SSD_GROUPS + g + 1) * SSD_STATE]
        bm_b = bm.astype(BF16)
        cm_b = cm.astype(BF16)
        cbm = _dot_nt(cm_b, bm_b)
        gs = slice(g * gw, (g + 1) * gw)

        h_prev = hst_ref[g]
        y_g = _dot(cm_b, h_prev.astype(BF16)) * dfs_full[:, gs]
        st = _dot(bm.T.astype(BF16), x_dte[:, gs])
        hst_ref[g] = h_prev * dfs_full[L - 1:L, gs] + st

        pieces = []
        for pi in range(heads_per_group // 2):
            ls = slice(g * gw + pi * LANES, g * gw + (pi + 1) * LANES)
            xp = x_dt_b[:, ls]
            yd = []
            for hd in range(2):
                h = g * heads_per_group + 2 * pi + hd
                seg = a_cs[:, h:h + 1] - a_cs_t[h:h + 1, :]
                dec = jnp.exp(jnp.where(causal, seg, MASK_NEG))
                yd.append(_dot((cbm * dec).astype(BF16), xp))
            pieces.append(jnp.where(first_head, yd[0], yd[1]))
        y_g = y_g + jnp.concatenate(pieces, axis=1) + xs[:, gs] * dskip_ref[:, gs]

        gz = y_g * _silu(z_ref[0, :, gs])
        ms = jnp.mean(gz * gz, axis=-1, keepdims=True)
        y_ref[0, :, gs] = gz * lax.rsqrt(ms + RMS_EPS) * nw_ref[:, gs]


def _dot_exact_rhs_lhs(lhs_bf16, x):
    hi, mid, lo = _split3(x)
    return _dot(lhs_bf16, hi) + _dot(lhs_bf16, mid) + _dot(lhs_bf16, lo)


def _ssd(h3d, conv_w, conv_b, dt_bias, a_log, d_full, norm_w):
    bsz, seq, _ = h3d.shape
    n_c = seq // SSD_CHUNK
    const = lambda b, c: (0, 0)
    return pl.pallas_call(
        _ssd_kernel,
        out_shape=jax.ShapeDtypeStruct((bsz, seq, SSD_INNER), F32),
        grid=(bsz, n_c),
        in_specs=[
            pl.BlockSpec((1, SSD_CHUNK, SSD_CONV_CH), lambda b, c: (b, c, COL_XBC // SSD_CONV_CH)),
            pl.BlockSpec((1, SSD_CHUNK, LANES), lambda b, c: (b, c, COL_DT // LANES)),
            pl.BlockSpec((1, SSD_CHUNK, SSD_INNER), lambda b, c: (b, c, COL_Z // SSD_INNER)),
            pl.BlockSpec((SSD_CONV, SSD_CONV_CH), const),
            pl.BlockSpec((1, SSD_CONV_CH), const),
            pl.BlockSpec((1, LANES), const),
            pl.BlockSpec((1, LANES), const),
            pl.BlockSpec((1, SSD_INNER), const),
            pl.BlockSpec((1, SSD_INNER), const),
        ],
        out_specs=pl.BlockSpec((1, SSD_CHUNK, SSD_INNER), lambda b, c: (b, c, 0)),
        scratch_shapes=[
            pltpu.VMEM((SSD_HALO + SSD_CHUNK, SSD_CONV_CH), F32),
            pltpu.VMEM((SSD_GROUPS, SSD_STATE, SSD_INNER // SSD_GROUPS), F32),
        ],
        compiler_params=_params("parallel", "arbitrary"),
        name="ssd_scan",
    )(h3d, h3d, h3d, conv_w, conv_b, dt_bias, a_log, d_full, norm_w)


def _ffn_ln_value(x, wg_ref, wu_ref, wd_ref, g_ref, b_ref):
    xb = x.astype(BF16)
    gate = _dot(xb, wg_ref[...])
    up = _dot(xb, wu_ref[...])
    ffn = _dot((_silu(gate) * up).astype(BF16), wd_ref[...])
    return _layer_norm(DN_ALPHA * x + ffn, g_ref[...], b_ref[...])


def _ffn_specs(hid, layer):
    def stacked(rows, cols):
        return pl.BlockSpec((None, rows, cols), lambda *_: (layer, 0, 0), pipeline_mode=pl.Buffered(1))
    return [stacked(D_MODEL, hid), stacked(D_MODEL, hid), stacked(hid, D_MODEL),
            _resident((1, D_MODEL)), _resident((1, D_MODEL))]


def _mix_ffn_kernel(attn_ref, y_ref, x_ref, wo_ref, g1_ref, b1_ref,
                    wg_ref, wu_ref, wd_ref, g2_ref, b2_ref, o_ref):
    mixed = (_dot(attn_ref[...].astype(BF16), wo_ref[0:ATTN_WIDTH, :])
             + _dot(y_ref[...].astype(BF16), wo_ref[ATTN_WIDTH:, :]))
    x1 = _layer_norm(DN_ALPHA * x_ref[...] + mixed, g1_ref[...], b1_ref[...])
    o_ref[...] = _ffn_ln_value(x1, wg_ref, wu_ref, wd_ref, g2_ref, b2_ref)


def _mix_ffn(attn2d, y2d, x2d, w_out, g1, b1, layer, w_gate, w_up, w_down, g2, b2):
    t = x2d.shape[0]
    return pl.pallas_call(
        _mix_ffn_kernel,
        out_shape=jax.ShapeDtypeStruct((t, D_MODEL), F32),
        grid=(t // ROW_TILE,),
        in_specs=[
            pl.BlockSpec((ROW_TILE, ATTN_WIDTH), lambda i: (i, 0)),
            pl.BlockSpec((ROW_TILE, SSD_INNER), lambda i: (i, 0)),
            pl.BlockSpec((ROW_TILE, D_MODEL), lambda i: (i, 0)),
            _resident((ATTN_WIDTH + SSD_INNER, D_MODEL)),
            _resident((1, D_MODEL)),
            _resident((1, D_MODEL)),
        ] + _ffn_specs(w_gate.shape[-1], layer),
        out_specs=pl.BlockSpec((ROW_TILE, D_MODEL), lambda i: (i, 0)),
        compiler_params=_params("parallel"),
        name="outproj_ffn_ln",
    )(attn2d, y2d, x2d, w_out, g1, b1, w_gate, w_up, w_down, g2, b2)


def _conformer_ffn_kernel(x_ref, w1_ref, b1_ref, dww_ref, dwb_ref, cg_ref, cb_ref, w2_ref, b2_ref,
                          g_ref, b_ref, wg_ref, wu_ref, wd_ref, g2_ref, b2f_ref, o_ref,
                          hpad_ref, shift_ref, cv_ref, x3_ref, *, tiles_per_seq):
    t = pl.program_id(0)
    ts = CONV_SEQ_TILE
    n_slabs = D_MODEL // LANES

    @pl.when(t == 0)
    def _():
        x3_ref[...] = jnp.zeros(x3_ref.shape, F32)

    @pl.when(t % tiles_per_seq == 0)
    def _():
        hpad_ref[:, 0:CONV_HALO, :] = jnp.zeros((n_slabs, CONV_HALO, LANES), F32)

    xin = x3_ref[(t + 1) % 2]
    xin_b = xin.astype(BF16)

    x = x_ref[...]
    xb = x.astype(BF16)
    val = _dot(xb, w1_ref[:, :D_MODEL]) + b1_ref[:, :D_MODEL]
    gate = _dot(xb, w1_ref[:, D_MODEL:]) + b1_ref[:, D_MODEL:]
    h = val * jax.nn.sigmoid(gate)
    for cc in range(n_slabs):
        hpad_ref[cc, CONV_HALO:CONV_HALO + ts, :] = h[:, cc * LANES:(cc + 1) * LANES]

    hid = wg_ref.shape[-1]
    ffn = None
    for cc in range(n_slabs):
        if cc % FFN_SLABS_PER_PIECE == 0:
            c0 = cc // FFN_SLABS_PER_PIECE * FFN_HID_PIECE
            c1 = min(c0 + FFN_HID_PIECE, hid)
            act = _silu(_dot(xin_b, wg_ref[:, c0:c1])) * _dot(xin_b, wu_ref[:, c0:c1])
            part = _dot(act.astype(BF16), wd_ref[c0:c1, :])
            ffn = part if ffn is None else ffn + part
        cs = slice(cc * LANES, (cc + 1) * LANES)
        shifted = shift_ref.at[cc % 2]
        for r in range(BF16_ROWS):
            lo = CONV_HALO - BF16_ROWS - r
            shifted[r] = hpad_ref[cc, lo:lo + BF16_ROWS + ts, :].astype(BF16)
        wts = jnp.stack([jnp.broadcast_to(dww_ref[k:k + 1, cs].astype(BF16), (CONV_ROW_CHUNK, LANES))
                         for k in range(CONF_KERNEL)])
        bias = jnp.broadcast_to(dwb_ref[:, cs], (CONV_ROW_CHUNK, LANES))

        def row_chunk(rc, carry, cc=cc, shifted=shifted, wts=wts, bias=bias):
            r0 = pl.multiple_of(rc * CONV_ROW_CHUNK, CONV_ROW_CHUNK)
            srcs = []
            for k in range(CONF_KERNEL):
                a, r = divmod(CONF_KERNEL - 1 - k, BF16_ROWS)
                srcs.append(shifted[r, pl.ds(r0 + BF16_ROWS - BF16_ROWS * a, CONV_ROW_CHUNK), :])
            prod = jnp.stack(srcs).astype(F32) * wts.astype(F32)
            cv_ref[cc, pl.ds(r0, CONV_ROW_CHUNK), :] = bias + jnp.sum(prod, axis=0)
            return carry

        lax.fori_loop(0, ts // CONV_ROW_CHUNK, row_chunk, 0)
    hpad_ref[:, 0:CONV_HALO, :] = hpad_ref[:, ts:ts + CONV_HALO, :]
    o_ref[...] = _layer_norm(DN_ALPHA * xin + ffn, g2_ref[...], b2f_ref[...])

    conv = jnp.concatenate([cv_ref[cc] for cc in range(n_slabs)], axis=1)
    hn = _silu(_layer_norm(conv, cg_ref[...], cb_ref[...]))
    mixed = _dot(hn.astype(BF16), w2_ref[...]) + b2_ref[...]
    x3_ref[t % 2] = _layer_norm(DN_ALPHA * x + mixed, g_ref[...], b_ref[...])


def _conformer_ffn(x2d, seq, w1, b1, dw_w, dw_b, cg, cb, w2, b2, g, b, layer, w_gate, w_up, w_down,
                   g2, b2f):
    t = x2d.shape[0]
    n_tiles = t // CONV_SEQ_TILE
    vec = _resident((1, D_MODEL))
    rows = (CONV_SEQ_TILE, D_MODEL)
    return pl.pallas_call(
        functools.partial(_conformer_ffn_kernel, tiles_per_seq=seq // CONV_SEQ_TILE),
        out_shape=jax.ShapeDtypeStruct((t, D_MODEL), F32),
        grid=(n_tiles + 1,),
        in_specs=[
            pl.BlockSpec(rows, lambda i: (jnp.minimum(i, n_tiles - 1), 0)),
            _resident((D_MODEL, 2 * D_MODEL)),
            _resident((1, 2 * D_MODEL)),
            _resident((CONV_HALO, D_MODEL)),
            vec, vec, vec,
            _resident((D_MODEL, D_MODEL)),
            vec, vec, vec,
        ] + _ffn_specs(w_gate.shape[-1], layer),
        out_specs=pl.BlockSpec(rows, lambda i: (jnp.maximum(i - 1, 0), 0)),
        scratch_shapes=[
            pltpu.VMEM((D_MODEL // LANES, CONV_HALO + CONV_SEQ_TILE, LANES), F32),
            pltpu.VMEM((2, BF16_ROWS, BF16_ROWS + CONV_SEQ_TILE, LANES), BF16),
            pltpu.VMEM((D_MODEL // LANES, CONV_SEQ_TILE, LANES), F32),
            pltpu.VMEM((2,) + rows, F32),
        ],
        compiler_params=_params("arbitrary"),
        name="conformer_ffn_ln",
    )(x2d, w1, b1, dw_w, dw_b, cg, cb, w2, b2, g, b, w_gate, w_up, w_down, g2, b2f)


def _row(v):
    return v.reshape(1, -1).astype(F32)


def _pad_lanes(v):
    return jnp.pad(v.reshape(1, -1).astype(F32), ((0, 0), (0, LANES - v.shape[-1])))


def kernel(x, mix_w_in, ssd_conv_w, ssd_conv_b, ssd_dt_bias, ssd_a_log, ssd_d, ssd_norm_w, mix_w_out, conv_w_pw1, conv_b_pw1, conv_dw_w, conv_dw_b, conv_ln_g, conv_ln_b, conv_w_pw2, conv_b_pw2, ffn_w_gate, ffn_w_up, ffn_w_down, ln_mix_g, ln_mix_b, ln_ffn_g, ln_ffn_b):
    bsz, seq, d = x.shape
    t = bsz * seq
    n_heads = ATTN_WIDTH // HEAD_DIM
    slopes = 2.0 ** (-8.0 * jnp.arange(1, n_heads + 1, dtype=F32) / n_heads)

    w_in = jnp.pad(mix_w_in[0], ((0, 0), (0, IN_COLS_PAD - mix_w_in.shape[-1]))).astype(BF16)
    h = _inproj(x.reshape(t, d), w_in).reshape(bsz, seq, IN_COLS_PAD)
    attn = _moba(h, slopes)
    y = _ssd(h, ssd_conv_w[0], _row(ssd_conv_b[0]), _pad_lanes(ssd_dt_bias[0]), _pad_lanes(ssd_a_log[0]),
             _row(jnp.repeat(ssd_d[0], HEAD_DIM)), _row(ssd_norm_w[0]))
    ffn_w = (ffn_w_gate.astype(BF16), ffn_w_up.astype(BF16), ffn_w_down.astype(BF16))
    x2 = _mix_ffn(attn.reshape(t, ATTN_WIDTH), y.reshape(t, SSD_INNER), x.reshape(t, d),
                  mix_w_out[0].astype(BF16), _row(ln_mix_g[0]), _row(ln_mix_b[0]),
                  0, *ffn_w, _row(ln_ffn_g[0]), _row(ln_ffn_b[0]))

    dw_w = jnp.pad(conv_dw_w[0], ((0, CONV_HALO - CONF_KERNEL), (0, 0)))
    x4 = _conformer_ffn(x2, seq, conv_w_pw1[0].astype(BF16), _row(conv_b_pw1[0]),
                        dw_w, _row(conv_dw_b[0]), _row(conv_ln_g[0]), _row(conv_ln_b[0]),
                        conv_w_pw2[0].astype(BF16), _row(conv_b_pw2[0]),
                        _row(ln_mix_g[1]), _row(ln_mix_b[1]),
                        1, *ffn_w, _row(ln_ffn_g[1]), _row(ln_ffn_b[1]))
    return x4.reshape(bsz, seq, d)
```

```python
import functools

import jax
import jax.numpy as jnp
from jax import lax
from jax.experimental import pallas as pl
from jax.experimental.pallas import tpu as pltpu

F32 = jnp.float32
BF16 = jnp.bfloat16

D_MODEL = 1024
DEPTH = 2
ATTN_WIDTH = 512
HEAD_DIM = 64
MOBA_BLOCK = 256
MOBA_TOPK = 3
SSD_INNER = 512
SSD_HEADS = 8
SSD_GROUPS = 2
SSD_STATE = 128
SSD_CONV = 4
SSD_CHUNK = 256
SSD_CONV_CH = SSD_INNER + 2 * SSD_GROUPS * SSD_STATE
CONF_KERNEL = 31
DN_ALPHA = (2 * DEPTH) ** 0.25
LN_EPS = 1e-5
RMS_EPS = 1e-5

LANES = 128
SUBLANES = 8

COL_Q = 0
COL_K = ATTN_WIDTH
COL_V = 2 * ATTN_WIDTH
COL_Z = 3 * ATTN_WIDTH
COL_XBC = COL_Z + SSD_INNER
COL_DT = COL_XBC + SSD_CONV_CH
IN_COLS_PAD = COL_DT + LANES

MASK_NEG = -1e30
LOG2_E = 1.4426950408889634

ROW_TILE = 512
INPROJ_COL_CHUNK = 512
CONV_SEQ_TILE = 512
CONV_HALO = 32
CONV_ROW_CHUNK = 128
FFN_HID_PIECE = 1024
SSD_HALO = 8
VMEM_LIMIT = 56 * 1024 * 1024


def _params(*sem):
    return pltpu.CompilerParams(dimension_semantics=sem, vmem_limit_bytes=VMEM_LIMIT)


def _dot(a, b):
    return jnp.dot(a, b, preferred_element_type=F32)


def _dot_nt(a, b):
    return lax.dot_general(a, b, (((1,), (1,)), ((), ())), preferred_element_type=F32)


def _split3(x):
    hi = x.astype(BF16)
    r1 = x - hi.astype(F32)
    mid = r1.astype(BF16)
    lo = (r1 - mid.astype(F32)).astype(BF16)
    return hi, mid, lo


def _silu(x):
    return x * jax.nn.sigmoid(x)


def _layer_norm(r, g, b):
    mu = jnp.mean(r, axis=-1, keepdims=True)
    d = r - mu
    var = jnp.mean(d * d, axis=-1, keepdims=True)
    return d * lax.rsqrt(var + LN_EPS) * g + b


def _inproj_kernel(x_ref, w_ref, o_ref):
    xb = x_ref[...].astype(BF16)
    for c0 in range(0, IN_COLS_PAD, INPROJ_COL_CHUNK):
        c1 = min(c0 + INPROJ_COL_CHUNK, IN_COLS_PAD)
        o_ref[:, c0:c1] = _dot(xb, w_ref[:, c0:c1])


def _resident(shape):
    return pl.BlockSpec(shape, lambda *_: (0,) * len(shape), pipeline_mode=pl.Buffered(1))


def _inproj(x2d, w_pad):
    t = x2d.shape[0]
    return pl.pallas_call(
        _inproj_kernel,
        out_shape=jax.ShapeDtypeStruct((t, IN_COLS_PAD), F32),
        grid=(t // ROW_TILE,),
        in_specs=[
            pl.BlockSpec((ROW_TILE, D_MODEL), lambda i: (i, 0)),
            _resident((D_MODEL, IN_COLS_PAD)),
        ],
        out_specs=pl.BlockSpec((ROW_TILE, IN_COLS_PAD), lambda i: (i, 0)),
        compiler_params=_params("parallel"),
        name="inproj",
    )(x2d, w_pad)


def _moba_kernel(slopes_ref, q_ref, k_ref, v_ref, o_ref, kb_ref, vt_ref, km_ref, al_ref, s_ref):
    hp = pl.program_id(1)
    nb = kb_ref.shape[0]
    blk = MOBA_BLOCK
    scale = HEAD_DIM ** -0.5 * LOG2_E

    for n in range(nb):
        kblk = k_ref[0, n * blk:(n + 1) * blk, :]
        kb_ref[n] = kblk.astype(BF16)
        km_ref[n:n + 1, :] = jnp.sum(kblk, axis=0, keepdims=True) * (1.0 / blk)
        vt_ref[n] = v_ref[0, n * blk:(n + 1) * blk, :].T.astype(BF16)

    lane = lax.broadcasted_iota(jnp.int32, (blk, LANES), 1)
    key_in = lax.broadcasted_iota(jnp.int32, (blk, blk), 0)
    qry_in = lax.broadcasted_iota(jnp.int32, (blk, blk), 1)
    dist0 = (qry_in - key_in).astype(F32)
    causal = qry_in >= key_in
    km = km_ref[...]
    km_hi = km.astype(BF16)
    km_lo = (km - km_hi.astype(F32)).astype(BF16)
    vt_rows = [slice(hd * HEAD_DIM, (hd + 1) * HEAD_DIM) for hd in range(2)]
    head_lanes = [(lane >= hd * HEAD_DIM) & (lane < (hd + 1) * HEAD_DIM) for hd in range(2)]
    slopes = [slopes_ref[2 * hp + hd] * LOG2_E for hd in range(2)]
    for hd in range(2):
        al = -slopes[hd] * dist0
        al_ref[hd, 0] = al
        al_ref[hd, 1] = jnp.where(causal, al, MASK_NEG)

    def scores(i, hd):
        slope = slopes[hd]
        q = q_ref[0, i * blk:(i + 1) * blk, :]
        q_f = jnp.where(head_lanes[hd], q, 0.0) * scale
        q_hi = q_f.astype(BF16)
        if i > MOBA_TOPK:
            q_lo = (q_f - q_hi.astype(F32)).astype(BF16)
            gate = (_dot_nt(km_hi, q_hi) + _dot_nt(km_lo, q_hi)
                    + _dot_nt(km_hi, q_lo))
            blk_id = lax.broadcasted_iota(jnp.int32, gate.shape, 0)
            rank = jnp.zeros(gate.shape, jnp.int32)
            for m in range(i):
                gm = gate[m:m + 1, :]
                beats = (gm > gate) | ((gm == gate) & (blk_id > m))
                rank = rank + jnp.where(beats & (blk_id != m), 1, 0)
            blk_dist = ((i - blk_id) * blk).astype(F32)
            rb = jnp.where(rank < MOBA_TOPK, -slope * blk_dist, MASK_NEG)
            row_bias = [rb[j:j + 1, :] for j in range(i)]
        else:
            row_bias = [-slope * float((i - j) * blk) for j in range(i)]
        sbuf = s_ref.at[i % 2, hd]
        s = _dot_nt(kb_ref[i], q_hi) + al_ref[hd, 1]
        sbuf[i] = s
        m_max = jnp.max(s, axis=0, keepdims=True)
        for j in range(i):
            s = _dot_nt(kb_ref[j], q_hi) + al_ref[hd, 0] + row_bias[j]
            sbuf[j] = s
            m_max = jnp.maximum(m_max, jnp.max(s, axis=0, keepdims=True))
        return m_max

    def values(i, hd, m_max):
        sbuf = s_ref.at[i % 2, hd]
        l_sum = jnp.zeros((1, blk), F32)
        acc = jnp.zeros((HEAD_DIM, blk), F32)
        for j in range(i + 1):
            p = jnp.exp2(sbuf[j] - m_max)
            l_sum = l_sum + jnp.sum(p, axis=0, keepdims=True)
            acc = acc + _dot(vt_ref[j, vt_rows[hd], :], p.astype(BF16))
        return acc / l_sum

    units = [(i, hd) for i in range(nb) for hd in range(2)]
    outs = {}
    m_next = scores(*units[0])
    for u, (i, hd) in enumerate(units):
        m_cur = m_next
        if u + 1 < len(units):
            m_next = scores(*units[u + 1])
        outs[hd] = values(i, hd, m_cur)
        if hd == 1:
            o_ref[0, i * blk:(i + 1) * blk, :] = jnp.concatenate([outs[0], outs[1]], axis=0).T


def _moba(h3d, slopes):
    bsz, seq, _ = h3d.shape
    nb = seq // MOBA_BLOCK
    n_pairs = ATTN_WIDTH // LANES
    return pl.pallas_call(
        _moba_kernel,
        out_shape=jax.ShapeDtypeStruct((bsz, seq, ATTN_WIDTH), F32),
        grid=(bsz, n_pairs),
        in_specs=[
            pl.BlockSpec(memory_space=pltpu.SMEM),
            pl.BlockSpec((1, seq, LANES), lambda b, p: (b, 0, COL_Q // LANES + p)),
            pl.BlockSpec((1, seq, LANES), lambda b, p: (b, 0, COL_K // LANES + p)),
            pl.BlockSpec((1, seq, LANES), lambda b, p: (b, 0, COL_V // LANES + p)),
        ],
        out_specs=pl.BlockSpec((1, seq, LANES), lambda b, p: (b, 0, p)),
        scratch_shapes=[
            pltpu.VMEM((nb, MOBA_BLOCK, LANES), BF16),
            pltpu.VMEM((nb, LANES, MOBA_BLOCK), BF16),
            pltpu.VMEM((nb, LANES), F32),
            pltpu.VMEM((2, 2, MOBA_BLOCK, MOBA_BLOCK), F32),
            pltpu.VMEM((2, 2, nb, MOBA_BLOCK, MOBA_BLOCK), F32),
        ],
        compiler_params=_params("parallel", "parallel"),
        name="moba_attention",
    )(slopes, h3d, h3d, h3d)


def _ssd_kernel(xbc_ref, dt_ref, z_ref, cw_ref, cb_ref, dtb_ref, alog_ref, dskip_ref, nw_ref,
                y_ref, xpad_ref, hst_ref):
    c = pl.program_id(1)
    L = SSD_CHUNK
    gw = SSD_INNER // SSD_GROUPS

    @pl.when(c == 0)
    def _():
        xpad_ref[0:SSD_HALO, :] = jnp.zeros((SSD_HALO, SSD_CONV_CH), F32)
        hst_ref[...] = jnp.zeros(hst_ref.shape, F32)

    xpad_ref[SSD_HALO:SSD_HALO + L, :] = xbc_ref[0]
    conv = jnp.broadcast_to(cb_ref[...], (L, SSD_CONV_CH))
    for k in range(SSD_CONV):
        off = SSD_HALO - (SSD_CONV - 1) + k
        conv = conv + cw_ref[k:k + 1, :] * xpad_ref[off:off + L, :]
    xpad_ref[0:SSD_HALO, :] = xpad_ref[L:L + SSD_HALO, :]
    xbc = _silu(conv)
    xs = xbc[:, :SSD_INNER]

    dt_in = dt_ref[0] + dtb_ref[...]
    dt = jnp.maximum(dt_in, 0.0) + jnp.log1p(jnp.exp(-jnp.abs(dt_in)))
    a = -jnp.exp(alog_ref[...])
    a_dt = dt * a
    row = lax.broadcasted_iota(jnp.int32, (L, L), 0)
    col = lax.broadcasted_iota(jnp.int32, (L, L), 1)
    causal = row >= col
    tril = jnp.where(causal, 1.0, 0.0).astype(BF16)
    a_cs = _dot_exact_rhs_lhs(tril, a_dt)
    a_cs_t = a_cs.T
    a_last = a_cs[L - 1:L, :]
    decay_to_end = jnp.exp(a_last - a_cs)
    decay_from_start = jnp.exp(a_cs)

    eh = lax.broadcasted_iota(jnp.int32, (2 * LANES, SSD_INNER), 0) & (LANES - 1)
    ec = lax.broadcasted_iota(jnp.int32, (2 * LANES, SSD_INNER), 1)
    expand = jnp.where((ec >= eh * HEAD_DIM) & (ec < (eh + 1) * HEAD_DIM), 1.0, 0.0).astype(BF16)
    per_head = jnp.concatenate([dt, decay_to_end, decay_from_start], axis=0)
    ph_hi = per_head.astype(BF16)
    ph_lo = (per_head - ph_hi.astype(F32)).astype(BF16)
    per_chan = _dot(jnp.concatenate([ph_hi, ph_lo], axis=1), expand)
    dt_full = per_chan[0:L]
    dte_full = per_chan[L:2 * L]
    dfs_full = per_chan[2 * L:3 * L]

    x_dt = xs * dt_full
    x_dte = (x_dt * dte_full).astype(BF16)
    x_dt_b = x_dt.astype(BF16)
    first_head = lax.broadcasted_iota(jnp.int32, (L, LANES), 1) < HEAD_DIM
    heads_per_group = SSD_HEADS // SSD_GROUPS

    for g in range(SSD_GROUPS):
        bm = xbc[:, SSD_INNER + g * SSD_STATE:SSD_INNER + (g + 1) * SSD_STATE]
        cm = xbc[:, SSD_INNER + (SSD_GROUPS + g) * SSD_STATE:SSD_INNER + (SSD_GROUPS + g + 1) * SSD_STATE]
        bm_b = bm.astype(BF16)
        cm_b = cm.astype(BF16)
        cbm = _dot_nt(cm_b, bm_b)
        gs = slice(g * gw, (g + 1) * gw)

        h_prev = hst_ref[g]
        y_g = _dot(cm_b, h_prev.astype(BF16)) * dfs_full[:, gs]
        st = _dot(bm.T.astype(BF16), x_dte[:, gs])
        hst_ref[g] = h_prev * dfs_full[L - 1:L, gs] + st

        pieces = []
        for pi in range(heads_per_group // 2):
            ls = slice(g * gw + pi * LANES, g * gw + (pi + 1) * LANES)
            xp = x_dt_b[:, ls]
            yd = []
            for hd in range(2):
                h = g * heads_per_group + 2 * pi + hd
                seg = a_cs[:, h:h + 1] - a_cs_t[h:h + 1, :]
                dec = jnp.exp(jnp.where(causal, seg, MASK_NEG))
                yd.append(_dot((cbm * dec).astype(BF16), xp))
            pieces.append(jnp.where(first_head, yd[0], yd[1]))
        y_g = y_g + jnp.concatenate(pieces, axis=1) + xs[:, gs] * dskip_ref[:, gs]

        gz = y_g * _silu(z_ref[0, :, gs])
        ms = jnp.mean(gz * gz, axis=-1, keepdims=True)
        y_ref[0, :, gs] = gz * lax.rsqrt(ms + RMS_EPS) * nw_ref[:, gs]


def _dot_exact_rhs_lhs(lhs_bf16, x):
    hi, mid, lo = _split3(x)
    return _dot(lhs_bf16, hi) + _dot(lhs_bf16, mid) + _dot(lhs_bf16, lo)


def _ssd(h3d, conv_w, conv_b, dt_bias, a_log, d_full, norm_w):
    bsz, seq, _ = h3d.shape
    n_c = seq // SSD_CHUNK
    const = lambda b, c: (0, 0)
    return pl.pallas_call(
        _ssd_kernel,
        out_shape=jax.ShapeDtypeStruct((bsz, seq, SSD_INNER), F32),
        grid=(bsz, n_c),
        in_specs=[
            pl.BlockSpec((1, SSD_CHUNK, SSD_CONV_CH), lambda b, c: (b, c, COL_XBC // SSD_CONV_CH)),
            pl.BlockSpec((1, SSD_CHUNK, LANES), lambda b, c: (b, c, COL_DT // LANES)),
            pl.BlockSpec((1, SSD_CHUNK, SSD_INNER), lambda b, c: (b, c, COL_Z // SSD_INNER)),
            pl.BlockSpec((SSD_CONV, SSD_CONV_CH), const),
            pl.BlockSpec((1, SSD_CONV_CH), const),
            pl.BlockSpec((1, LANES), const),
            pl.BlockSpec((1, LANES), const),
            pl.BlockSpec((1, SSD_INNER), const),
            pl.BlockSpec((1, SSD_INNER), const),
        ],
        out_specs=pl.BlockSpec((1, SSD_CHUNK, SSD_INNER), lambda b, c: (b, c, 0)),
        scratch_shapes=[
            pltpu.VMEM((SSD_HALO + SSD_CHUNK, SSD_CONV_CH), F32),
            pltpu.VMEM((SSD_GROUPS, SSD_STATE, SSD_INNER // SSD_GROUPS), F32),
        ],
        compiler_params=_params("parallel", "arbitrary"),
        name="ssd_scan",
    )(h3d, h3d, h3d, conv_w, conv_b, dt_bias, a_log, d_full, norm_w)


def _ffn_ln_value(x, wg_ref, wu_ref, wd_ref, g_ref, b_ref):
    xb = x.astype(BF16)
    gate = _dot(xb, wg_ref[...])
    up = _dot(xb, wu_ref[...])
    ffn = _dot((_silu(gate) * up).astype(BF16), wd_ref[...])
    return _layer_norm(DN_ALPHA * x + ffn, g_ref[...], b_ref[...])


def _ffn_specs(hid, layer):
    def stacked(rows, cols):
        return pl.BlockSpec((None, rows, cols), lambda *_: (layer, 0, 0), pipeline_mode=pl.Buffered(1))
    return [stacked(D_MODEL, hid), stacked(D_MODEL, hid), stacked(hid, D_MODEL),
            _resident((1, D_MODEL)), _resident((1, D_MODEL))]


def _mix_ffn_kernel(attn_ref, y_ref, x_ref, wo_ref, g1_ref, b1_ref,
                    wg_ref, wu_ref, wd_ref, g2_ref, b2_ref, o_ref):
    half = ROW_TILE // 2
    rows = [slice(0, half), slice(half, ROW_TILE)]

    def mixed(rs):
        return (_dot(attn_ref[rs, :].astype(BF16), wo_ref[0:ATTN_WIDTH, :])
                + _dot(y_ref[rs, :].astype(BF16), wo_ref[ATTN_WIDTH:, :]))

    def norm1(rs, m):
        return _layer_norm(DN_ALPHA * x_ref[rs, :] + m, g1_ref[...], b1_ref[...])

    def act(x1):
        xb = x1.astype(BF16)
        return (_silu(_dot(xb, wg_ref[...])) * _dot(xb, wu_ref[...])).astype(BF16)

    def norm2(x1, a):
        return _layer_norm(DN_ALPHA * x1 + _dot(a, wd_ref[...]), g2_ref[...], b2_ref[...])

    m_a, m_b = mixed(rows[0]), mixed(rows[1])
    x1_a = norm1(rows[0], m_a)
    act_a = act(x1_a)
    x1_b = norm1(rows[1], m_b)
    act_b = act(x1_b)
    o_ref[rows[0], :] = norm2(x1_a, act_a)
    o_ref[rows[1], :] = norm2(x1_b, act_b)


def _mix_ffn(attn2d, y2d, x2d, w_out, g1, b1, layer, w_gate, w_up, w_down, g2, b2):
    t = x2d.shape[0]
    return pl.pallas_call(
        _mix_ffn_kernel,
        out_shape=jax.ShapeDtypeStruct((t, D_MODEL), F32),
        grid=(t // ROW_TILE,),
        in_specs=[
            pl.BlockSpec((ROW_TILE, ATTN_WIDTH), lambda i: (i, 0)),
            pl.BlockSpec((ROW_TILE, SSD_INNER), lambda i: (i, 0)),
            pl.BlockSpec((ROW_TILE, D_MODEL), lambda i: (i, 0)),
            _resident((ATTN_WIDTH + SSD_INNER, D_MODEL)),
            _resident((1, D_MODEL)),
            _resident((1, D_MODEL)),
        ] + _ffn_specs(w_gate.shape[-1], layer),
        out_specs=pl.BlockSpec((ROW_TILE, D_MODEL), lambda i: (i, 0)),
        compiler_params=_params("parallel"),
        name="outproj_ffn_ln",
    )(attn2d, y2d, x2d, w_out, g1, b1, w_gate, w_up, w_down, g2, b2)


def _conformer_ffn_kernel(x_ref, w1_ref, b1_ref, dww_ref, dwb_ref, cg_ref, cb_ref, w2_ref, b2_ref,
                          g_ref, b_ref, wg_ref, wu_ref, wd_ref, g2_ref, b2f_ref, o_ref,
                          hpad_ref, shift_ref, cv_ref, x3_ref, *, tiles_per_seq):
    t = pl.program_id(0)
    ts = CONV_SEQ_TILE
    lead = CONV_HALO - SUBLANES
    n_slabs = D_MODEL // LANES

    @pl.when(t == 0)
    def _():
        x3_ref[...] = jnp.zeros(x3_ref.shape, F32)

    @pl.when(t % tiles_per_seq == 0)
    def _():
        hpad_ref[:, 0:CONV_HALO, :] = jnp.zeros((n_slabs, CONV_HALO, LANES), F32)

    xin = x3_ref[(t + 1) % 2]
    xin_b = xin.astype(BF16)

    x = x_ref[...]
    xb = x.astype(BF16)
    val = _dot(xb, w1_ref[:, :D_MODEL]) + b1_ref[:, :D_MODEL]
    gate = _dot(xb, w1_ref[:, D_MODEL:]) + b1_ref[:, D_MODEL:]
    h = val * jax.nn.sigmoid(gate)
    for cc in range(n_slabs):
        hpad_ref[cc, CONV_HALO:CONV_HALO + ts, :] = h[:, cc * LANES:(cc + 1) * LANES]

    tiles = CONV_ROW_CHUNK // SUBLANES
    hid = wg_ref.shape[-1]

    def ffn_piece(c0):
        c1 = min(c0 + FFN_HID_PIECE, hid)
        act = _silu(_dot(xin_b, wg_ref[:, c0:c1])) * _dot(xin_b, wu_ref[:, c0:c1])
        return _dot(act.astype(BF16), wd_ref[c0:c1, :])

    for cc in range(n_slabs):
        cs = slice(cc * LANES, (cc + 1) * LANES)
        shifted = shift_ref.at[cc % 2]
        for r in range(1, SUBLANES):
            shifted[r - 1] = hpad_ref[cc, SUBLANES - r:SUBLANES - r + lead + ts, :]
        taps = [jnp.broadcast_to(dww_ref[k:k + 1, cs], (SUBLANES, LANES)) for k in range(CONF_KERNEL)]
        bias = jnp.broadcast_to(dwb_ref[:, cs], (SUBLANES, LANES))
        for r0 in range(0, ts, CONV_ROW_CHUNK):
            acc = jnp.broadcast_to(bias[None], (tiles, SUBLANES, LANES))
            for k in range(CONF_KERNEL):
                a, r = divmod(CONF_KERNEL - 1 - k, SUBLANES)
                if r == 0:
                    lo = r0 + CONV_HALO - SUBLANES * a
                    src = hpad_ref[cc, lo:lo + CONV_ROW_CHUNK, :]
                else:
                    lo = r0 + lead - SUBLANES * a
                    src = shifted[r - 1, lo:lo + CONV_ROW_CHUNK, :]
                acc = acc + taps[k][None] * src.reshape(tiles, SUBLANES, LANES)
            cv_ref[cc, r0:r0 + CONV_ROW_CHUNK, :] = acc.reshape(CONV_ROW_CHUNK, LANES)
    hpad_ref[:, 0:CONV_HALO, :] = hpad_ref[:, ts:ts + CONV_HALO, :]

    ffn = ffn_piece(0)
    conv = jnp.concatenate([cv_ref[cc] for cc in range(n_slabs)], axis=1)
    hn = _silu(_layer_norm(conv, cg_ref[...], cb_ref[...]))
    ffn = ffn + ffn_piece(FFN_HID_PIECE)
    mixed = _dot(hn.astype(BF16), w2_ref[...]) + b2_ref[...]
    ffn = ffn + ffn_piece(2 * FFN_HID_PIECE)
    x3_ref[t % 2] = _layer_norm(DN_ALPHA * x + mixed, g_ref[...], b_ref[...])
    o_ref[...] = _layer_norm(DN_ALPHA * xin + ffn, g2_ref[...], b2f_ref[...])


def _conformer_ffn(x2d, seq, w1, b1, dw_w, dw_b, cg, cb, w2, b2, g, b, layer, w_gate, w_up, w_down,
                   g2, b2f):
    t = x2d.shape[0]
    n_tiles = t // CONV_SEQ_TILE
    assert 2 * FFN_HID_PIECE < w_gate.shape[-1] <= 3 * FFN_HID_PIECE
    vec = _resident((1, D_MODEL))
    rows = (CONV_SEQ_TILE, D_MODEL)
    return pl.pallas_call(
        functools.partial(_conformer_ffn_kernel, tiles_per_seq=seq // CONV_SEQ_TILE),
        out_shape=jax.ShapeDtypeStruct((t, D_MODEL), F32),
        grid=(n_tiles + 1,),
        in_specs=[
            pl.BlockSpec(rows, lambda i: (jnp.minimum(i, n_tiles - 1), 0)),
            _resident((D_MODEL, 2 * D_MODEL)),
            _resident((1, 2 * D_MODEL)),
            _resident((CONV_HALO, D_MODEL)),
            vec, vec, vec,
            _resident((D_MODEL, D_MODEL)),
            vec, vec, vec,
        ] + _ffn_specs(w_gate.shape[-1], layer),
        out_specs=pl.BlockSpec(rows, lambda i: (jnp.maximum(i - 1, 0), 0)),
        scratch_shapes=[
            pltpu.VMEM((D_MODEL // LANES, CONV_HALO + CONV_SEQ_TILE, LANES), F32),
            pltpu.VMEM((2, SUBLANES - 1, CONV_HALO - SUBLANES + CONV_SEQ_TILE, LANES), F32),
            pltpu.VMEM((D_MODEL // LANES, CONV_SEQ_TILE, LANES), F32),
            pltpu.VMEM((2,) + rows, F32),
        ],
        compiler_params=_params("arbitrary"),
        name="conformer_ffn_ln",
    )(x2d, w1, b1, dw_w, dw_b, cg, cb, w2, b2, g, b, w_gate, w_up, w_down, g2, b2f)


def _row(v):
    return v.reshape(1, -1).astype(F32)


def _pad_lanes(v):
    return jnp.pad(v.reshape(1, -1).astype(F32), ((0, 0), (0, LANES - v.shape[-1])))


def kernel(x, mix_w_in, ssd_conv_w, ssd_conv_b, ssd_dt_bias, ssd_a_log, ssd_d, ssd_norm_w, mix_w_out, conv_w_pw1, conv_b_pw1, conv_dw_w, conv_dw_b, conv_ln_g, conv_ln_b, conv_w_pw2, conv_b_pw2, ffn_w_gate, ffn_w_up, ffn_w_down, ln_mix_g, ln_mix_b, ln_ffn_g, ln_ffn_b):
    bsz, seq, d = x.shape
    t = bsz * seq
    n_heads = ATTN_WIDTH // HEAD_DIM
    slopes = 2.0 ** (-8.0 * jnp.arange(1, n_heads + 1, dtype=F32) / n_heads)

    w_in = jnp.pad(mix_w_in[0], ((0, 0), (0, IN_COLS_PAD - mix_w_in.shape[-1]))).astype(BF16)
    h = _inproj(x.reshape(t, d), w_in).reshape(bsz, seq, IN_COLS_PAD)
    attn = _moba(h, slopes)
    y = _ssd(h, ssd_conv_w[0], _row(ssd_conv_b[0]), _pad_lanes(ssd_dt_bias[0]), _pad_lanes(ssd_a_log[0]),
             _row(jnp.repeat(ssd_d[0], HEAD_DIM)), _row(ssd_norm_w[0]))
    ffn_w = (ffn_w_gate.astype(BF16), ffn_w_up.astype(BF16), ffn_w_down.astype(BF16))
    x2 = _mix_ffn(attn.reshape(t, ATTN_WIDTH), y.reshape(t, SSD_INNER), x.reshape(t, d),
                  mix_w_out[0].astype(BF16), _row(ln_mix_g[0]), _row(ln_mix_b[0]),
                  0, *ffn_w, _row(ln_ffn_g[0]), _row(ln_ffn_b[0]))

    dw_w = jnp.pad(conv_dw_w[0], ((0, CONV_HALO - CONF_KERNEL), (0, 0)))
    x4 = _conformer_ffn(x2, seq, conv_w_pw1[0].astype(BF16), _row(conv_b_pw1[0]),
                        dw_w, _row(conv_dw_b[0]), _row(conv_ln_g[0]), _row(conv_ln_b[0]),
                        conv_w_pw2[0].astype(BF16), _row(conv_b_pw2[0]),
                        _row(ln_mix_g[1]), _row(ln_mix_b[1]),
                        1, *ffn_w, _row(ln_ffn_g[1]), _row(ln_ffn_b[1]))
    return x4.reshape(bsz, seq, d)
```

```python
import functools

import jax
import jax.numpy as jnp
from jax import lax
from jax.experimental import pallas as pl
from jax.experimental.pallas import tpu as pltpu

F32 = jnp.float32
BF16 = jnp.bfloat16

D_MODEL = 1024
DEPTH = 2
ATTN_WIDTH = 512
HEAD_DIM = 64
MOBA_BLOCK = 256
MOBA_TOPK = 3
SSD_INNER = 512
SSD_HEADS = 8
SSD_GROUPS = 2
SSD_STATE = 128
SSD_CONV = 4
SSD_CHUNK = 256
SSD_CONV_CH = SSD_INNER + 2 * SSD_GROUPS * SSD_STATE
CONF_KERNEL = 31
DN_ALPHA = (2 * DEPTH) ** 0.25
LN_EPS = 1e-5
RMS_EPS = 1e-5

LANES = 128
SUBLANES = 8

COL_Q = 0
COL_K = ATTN_WIDTH
COL_V = 2 * ATTN_WIDTH
COL_Z = 3 * ATTN_WIDTH
COL_XBC = COL_Z + SSD_INNER
COL_DT = COL_XBC + SSD_CONV_CH
IN_COLS_PAD = COL_DT + LANES

MASK_NEG = -1e30
LOG2_E = 1.4426950408889634

ROW_TILE = 512
INPROJ_COL_CHUNK = 512
CONV_SEQ_TILE = 512
CONV_HALO = 32
CONV_ROW_CHUNK = 128
FFN_HID_PIECE = 1024
MOBA_SKEW = 2
SSD_HALO = 8
VMEM_LIMIT = 56 * 1024 * 1024


def _params(*sem):
    return pltpu.CompilerParams(dimension_semantics=sem, vmem_limit_bytes=VMEM_LIMIT)


def _dot(a, b):
    return jnp.dot(a, b, preferred_element_type=F32)


def _dot_nt(a, b):
    return lax.dot_general(a, b, (((1,), (1,)), ((), ())), preferred_element_type=F32)


def _split3(x):
    hi = x.astype(BF16)
    r1 = x - hi.astype(F32)
    mid = r1.astype(BF16)
    lo = (r1 - mid.astype(F32)).astype(BF16)
    return hi, mid, lo


def _silu(x):
    return x * jax.nn.sigmoid(x)


def _layer_norm(r, g, b):
    mu = jnp.mean(r, axis=-1, keepdims=True)
    d = r - mu
    var = jnp.mean(d * d, axis=-1, keepdims=True)
    return d * lax.rsqrt(var + LN_EPS) * g + b


def _inproj_kernel(x_ref, w_ref, o_ref):
    xb = x_ref[...].astype(BF16)
    for c0 in range(0, IN_COLS_PAD, INPROJ_COL_CHUNK):
        c1 = min(c0 + INPROJ_COL_CHUNK, IN_COLS_PAD)
        o_ref[:, c0:c1] = _dot(xb, w_ref[:, c0:c1])


def _resident(shape):
    return pl.BlockSpec(shape, lambda *_: (0,) * len(shape), pipeline_mode=pl.Buffered(1))


def _inproj(x2d, w_pad):
    t = x2d.shape[0]
    return pl.pallas_call(
        _inproj_kernel,
        out_shape=jax.ShapeDtypeStruct((t, IN_COLS_PAD), F32),
        grid=(t // ROW_TILE,),
        in_specs=[
            pl.BlockSpec((ROW_TILE, D_MODEL), lambda i: (i, 0)),
            _resident((D_MODEL, IN_COLS_PAD)),
        ],
        out_specs=pl.BlockSpec((ROW_TILE, IN_COLS_PAD), lambda i: (i, 0)),
        compiler_params=_params("parallel"),
        name="inproj",
    )(x2d, w_pad)


def _moba_kernel(slopes_ref, q_ref, k_ref, v_ref, o_ref, kb_ref, vt_ref, km_ref, al_ref, s_ref):
    hp = pl.program_id(1)
    nb = kb_ref.shape[0]
    blk = MOBA_BLOCK
    scale = HEAD_DIM ** -0.5 * LOG2_E

    for n in range(nb):
        kblk = k_ref[0, n * blk:(n + 1) * blk, :]
        kb_ref[n] = kblk.astype(BF16)
        km_ref[n:n + 1, :] = jnp.sum(kblk, axis=0, keepdims=True) * (1.0 / blk)
        vt_ref[n] = v_ref[0, n * blk:(n + 1) * blk, :].T.astype(BF16)

    lane = lax.broadcasted_iota(jnp.int32, (blk, LANES), 1)
    key_in = lax.broadcasted_iota(jnp.int32, (blk, blk), 0)
    qry_in = lax.broadcasted_iota(jnp.int32, (blk, blk), 1)
    dist0 = (qry_in - key_in).astype(F32)
    causal = qry_in >= key_in
    km = km_ref[...]
    km_hi = km.astype(BF16)
    km_lo = (km - km_hi.astype(F32)).astype(BF16)
    vt_rows = [slice(hd * HEAD_DIM, (hd + 1) * HEAD_DIM) for hd in range(2)]
    head_lanes = [(lane >= hd * HEAD_DIM) & (lane < (hd + 1) * HEAD_DIM) for hd in range(2)]
    slopes = [slopes_ref[2 * hp + hd] * LOG2_E for hd in range(2)]
    for hd in range(2):
        al = -slopes[hd] * dist0
        al_ref[hd, 0] = al
        al_ref[hd, 1] = jnp.where(causal, al, MASK_NEG)

    def scores(i, hd):
        slope = slopes[hd]
        q = q_ref[0, i * blk:(i + 1) * blk, :]
        q_f = jnp.where(head_lanes[hd], q, 0.0) * scale
        q_hi = q_f.astype(BF16)
        if i > MOBA_TOPK:
            q_lo = (q_f - q_hi.astype(F32)).astype(BF16)
            gate = (_dot_nt(km_hi, q_hi) + _dot_nt(km_lo, q_hi)
                    + _dot_nt(km_hi, q_lo))
            blk_id = lax.broadcasted_iota(jnp.int32, gate.shape, 0)
            rank = jnp.zeros(gate.shape, jnp.int32)
            for m in range(i):
                gm = gate[m:m + 1, :]
                beats = (gm > gate) | ((gm == gate) & (blk_id > m))
                rank = rank + jnp.where(beats & (blk_id != m), 1, 0)
            blk_dist = ((i - blk_id) * blk).astype(F32)
            rb = jnp.where(rank < MOBA_TOPK, -slope * blk_dist, MASK_NEG)
            row_bias = [rb[j:j + 1, :] for j in range(i)]
        else:
            row_bias = [-slope * float((i - j) * blk) for j in range(i)]
        sbuf = s_ref.at[i % 2, hd]
        s = _dot_nt(kb_ref[i], q_hi) + al_ref[hd, 1]
        sbuf[i] = s
        m_max = jnp.max(s, axis=0, keepdims=True)
        for j in range(i):
            s = _dot_nt(kb_ref[j], q_hi) + al_ref[hd, 0] + row_bias[j]
            sbuf[j] = s
            m_max = jnp.maximum(m_max, jnp.max(s, axis=0, keepdims=True))
        return m_max

    def values(i, hd, m_max):
        sbuf = s_ref.at[i % 2, hd]
        l_sum = jnp.zeros((1, blk), F32)
        acc = jnp.zeros((HEAD_DIM, blk), F32)
        for j in range(i + 1):
            p = jnp.exp2(sbuf[j] - m_max)
            l_sum = l_sum + jnp.sum(p, axis=0, keepdims=True)
            acc = acc + _dot(vt_ref[j, vt_rows[hd], :], p.astype(BF16))
        return acc / l_sum

    units = [(i, hd) for i in range(nb) for hd in range(2)]
    outs = {}
    maxima = [scores(*unit) for unit in units[:MOBA_SKEW]]
    for u, (i, hd) in enumerate(units):
        if u + MOBA_SKEW < len(units):
            maxima.append(scores(*units[u + MOBA_SKEW]))
        outs[hd] = values(i, hd, maxima[u])
        if hd == 1:
            o_ref[0, i * blk:(i + 1) * blk, :] = jnp.concatenate([outs[0], outs[1]], axis=0).T


def _moba(h3d, slopes):
    bsz, seq, _ = h3d.shape
    nb = seq // MOBA_BLOCK
    n_pairs = ATTN_WIDTH // LANES
    return pl.pallas_call(
        _moba_kernel,
        out_shape=jax.ShapeDtypeStruct((bsz, seq, ATTN_WIDTH), F32),
        grid=(bsz, n_pairs),
        in_specs=[
            pl.BlockSpec(memory_space=pltpu.SMEM),
            pl.BlockSpec((1, seq, LANES), lambda b, p: (b, 0, COL_Q // LANES + p)),
            pl.BlockSpec((1, seq, LANES), lambda b, p: (b, 0, COL_K // LANES + p)),
            pl.BlockSpec((1, seq, LANES), lambda b, p: (b, 0, COL_V // LANES + p)),
        ],
        out_specs=pl.BlockSpec((1, seq, LANES), lambda b, p: (b, 0, p)),
        scratch_shapes=[
            pltpu.VMEM((nb, MOBA_BLOCK, LANES), BF16),
            pltpu.VMEM((nb, LANES, MOBA_BLOCK), BF16),
            pltpu.VMEM((nb, LANES), F32),
            pltpu.VMEM((2, 2, MOBA_BLOCK, MOBA_BLOCK), F32),
            pltpu.VMEM((2, 2, nb, MOBA_BLOCK, MOBA_BLOCK), F32),
        ],
        compiler_params=_params("parallel", "parallel"),
        name="moba_attention",
    )(slopes, h3d, h3d, h3d)


def _ssd_kernel(xbc_ref, dt_ref, z_ref, cw_ref, cb_ref, dtb_ref, alog_ref, dskip_ref, nw_ref,
                y_ref, xpad_ref, hst_ref):
    c = pl.program_id(1)
    L = SSD_CHUNK
    gw = SSD_INNER // SSD_GROUPS

    @pl.when(c == 0)
    def _():
        xpad_ref[:, 0:SSD_HALO, :] = jnp.zeros((SSD_CONV_CH // LANES, SSD_HALO, LANES), F32)
        hst_ref[...] = jnp.zeros(hst_ref.shape, F32)

    conv_slabs = []
    for cc in range(SSD_CONV_CH // LANES):
        cs = slice(cc * LANES, (cc + 1) * LANES)
        xpad_ref[cc, SSD_HALO:SSD_HALO + L, :] = xbc_ref[0, :, cs]
        acc = jnp.broadcast_to(cb_ref[:, cs], (L, LANES))
        for k in range(SSD_CONV):
            off = SSD_HALO - (SSD_CONV - 1) + k
            acc = acc + cw_ref[k:k + 1, cs] * xpad_ref[cc, off:off + L, :]
        conv_slabs.append(acc)
    xpad_ref[:, 0:SSD_HALO, :] = xpad_ref[:, L:L + SSD_HALO, :]
    conv = jnp.concatenate(conv_slabs, axis=1)
    xbc = _silu(conv)
    xs = xbc[:, :SSD_INNER]

    dt_in = dt_ref[0] + dtb_ref[...]
    dt = jnp.maximum(dt_in, 0.0) + jnp.log1p(jnp.exp(-jnp.abs(dt_in)))
    a = -jnp.exp(alog_ref[...])
    a_dt = dt * a
    row = lax.broadcasted_iota(jnp.int32, (L, L), 0)
    col = lax.broadcasted_iota(jnp.int32, (L, L), 1)
    causal = row >= col
    tril = jnp.where(causal, 1.0, 0.0).astype(BF16)
    a_cs = _dot_exact_rhs_lhs(tril, a_dt)
    a_cs_t = a_cs.T
    a_last = a_cs[L - 1:L, :]
    decay_to_end = jnp.exp(a_last - a_cs)
    decay_from_start = jnp.exp(a_cs)

    eh = lax.broadcasted_iota(jnp.int32, (2 * LANES, SSD_INNER), 0) & (LANES - 1)
    ec = lax.broadcasted_iota(jnp.int32, (2 * LANES, SSD_INNER), 1)
    expand = jnp.where((ec >= eh * HEAD_DIM) & (ec < (eh + 1) * HEAD_DIM), 1.0, 0.0).astype(BF16)
    per_head = jnp.concatenate([dt, decay_to_end, decay_from_start], axis=0)
    ph_hi = per_head.astype(BF16)
    ph_lo = (per_head - ph_hi.astype(F32)).astype(BF16)
    per_chan = _dot(jnp.concatenate([ph_hi, ph_lo], axis=1), expand)
    dt_full = per_chan[0:L]
    dte_full = per_chan[L:2 * L]
    dfs_full = per_chan[2 * L:3 * L]

    x_dt = xs * dt_full
    x_dte = (x_dt * dte_full).astype(BF16)
    x_dt_b = x_dt.astype(BF16)
    first_head = lax.broadcasted_iota(jnp.int32, (L, LANES), 1) < HEAD_DIM
    heads_per_group = SSD_HEADS // SSD_GROUPS

    for g in range(SSD_GROUPS):
        bm = xbc[:, SSD_INNER + g * SSD_STATE:SSD_INNER + (g + 1) * SSD_STATE]
        cm = xbc[:, SSD_INNER + (SSD_GROUPS + g) * SSD_STATE:SSD_INNER + (SSD_GROUPS + g + 1) * SSD_STATE]
        bm_b = bm.astype(BF16)
        cm_b = cm.astype(BF16)
        cbm = _dot_nt(cm_b, bm_b)
        gs = slice(g * gw, (g + 1) * gw)

        h_prev = hst_ref[g]
        y_g = _dot(cm_b, h_prev.astype(BF16)) * dfs_full[:, gs]
        st = _dot(bm.T.astype(BF16), x_dte[:, gs])
        hst_ref[g] = h_prev * dfs_full[L - 1:L, gs] + st

        pieces = []
        for pi in range(heads_per_group // 2):
            ls = slice(g * gw + pi * LANES, g * gw + (pi + 1) * LANES)
            xp = x_dt_b[:, ls]
            yd = []
            for hd in range(2):
                h = g * heads_per_group + 2 * pi + hd
                seg = a_cs[:, h:h + 1] - a_cs_t[h:h + 1, :]
                dec = jnp.exp(jnp.where(causal, seg, MASK_NEG))
                yd.append(_dot((cbm * dec).astype(BF16), xp))
            pieces.append(jnp.where(first_head, yd[0], yd[1]))
        y_g = y_g + jnp.concatenate(pieces, axis=1) + xs[:, gs] * dskip_ref[:, gs]

        gz = y_g * _silu(z_ref[0, :, gs])
        ms = jnp.mean(gz * gz, axis=-1, keepdims=True)
        y_ref[0, :, gs] = gz * lax.rsqrt(ms + RMS_EPS) * nw_ref[:, gs]


def _dot_exact_rhs_lhs(lhs_bf16, x):
    hi, mid, lo = _split3(x)
    return _dot(lhs_bf16, hi) + _dot(lhs_bf16, mid) + _dot(lhs_bf16, lo)


def _ssd(h3d, conv_w, conv_b, dt_bias, a_log, d_full, norm_w):
    bsz, seq, _ = h3d.shape
    n_c = seq // SSD_CHUNK
    const = lambda b, c: (0, 0)
    return pl.pallas_call(
        _ssd_kernel,
        out_shape=jax.ShapeDtypeStruct((bsz, seq, SSD_INNER), F32),
        grid=(bsz, n_c),
        in_specs=[
            pl.BlockSpec((1, SSD_CHUNK, SSD_CONV_CH), lambda b, c: (b, c, COL_XBC // SSD_CONV_CH)),
            pl.BlockSpec((1, SSD_CHUNK, LANES), lambda b, c: (b, c, COL_DT // LANES)),
            pl.BlockSpec((1, SSD_CHUNK, SSD_INNER), lambda b, c: (b, c, COL_Z // SSD_INNER)),
            pl.BlockSpec((SSD_CONV, SSD_CONV_CH), const),
            pl.BlockSpec((1, SSD_CONV_CH), const),
            pl.BlockSpec((1, LANES), const),
            pl.BlockSpec((1, LANES), const),
            pl.BlockSpec((1, SSD_INNER), const),
            pl.BlockSpec((1, SSD_INNER), const),
        ],
        out_specs=pl.BlockSpec((1, SSD_CHUNK, SSD_INNER), lambda b, c: (b, c, 0)),
        scratch_shapes=[
            pltpu.VMEM((SSD_CONV_CH // LANES, SSD_HALO + SSD_CHUNK, LANES), F32),
            pltpu.VMEM((SSD_GROUPS, SSD_STATE, SSD_INNER // SSD_GROUPS), F32),
        ],
        compiler_params=_params("parallel", "arbitrary"),
        name="ssd_scan",
    )(h3d, h3d, h3d, conv_w, conv_b, dt_bias, a_log, d_full, norm_w)


def _ffn_ln_value(x, wg_ref, wu_ref, wd_ref, g_ref, b_ref):
    xb = x.astype(BF16)
    gate = _dot(xb, wg_ref[...])
    up = _dot(xb, wu_ref[...])
    ffn = _dot((_silu(gate) * up).astype(BF16), wd_ref[...])
    return _layer_norm(DN_ALPHA * x + ffn, g_ref[...], b_ref[...])


def _ffn_specs(hid, layer):
    def stacked(rows, cols):
        return pl.BlockSpec((None, rows, cols), lambda *_: (layer, 0, 0), pipeline_mode=pl.Buffered(1))
    return [stacked(D_MODEL, hid), stacked(D_MODEL, hid), stacked(hid, D_MODEL),
            _resident((1, D_MODEL)), _resident((1, D_MODEL))]


def _mix_ffn_kernel(attn_ref, y_ref, x_ref, wo_ref, g1_ref, b1_ref,
                    wg_ref, wu_ref, wd_ref, g2_ref, b2_ref, o_ref):
    half = ROW_TILE // 2
    rows = [slice(0, half), slice(half, ROW_TILE)]

    def mixed(rs):
        return (_dot(attn_ref[rs, :].astype(BF16), wo_ref[0:ATTN_WIDTH, :])
                + _dot(y_ref[rs, :].astype(BF16), wo_ref[ATTN_WIDTH:, :]))

    def norm1(rs, m):
        return _layer_norm(DN_ALPHA * x_ref[rs, :] + m, g1_ref[...], b1_ref[...])

    def act(x1):
        xb = x1.astype(BF16)
        return (_silu(_dot(xb, wg_ref[...])) * _dot(xb, wu_ref[...])).astype(BF16)

    def norm2(x1, a):
        return _layer_norm(DN_ALPHA * x1 + _dot(a, wd_ref[...]), g2_ref[...], b2_ref[...])

    m_a, m_b = mixed(rows[0]), mixed(rows[1])
    x1_a = norm1(rows[0], m_a)
    act_a = act(x1_a)
    x1_b = norm1(rows[1], m_b)
    act_b = act(x1_b)
    o_ref[rows[0], :] = norm2(x1_a, act_a)
    o_ref[rows[1], :] = norm2(x1_b, act_b)


def _mix_ffn(attn2d, y2d, x2d, w_out, g1, b1, layer, w_gate, w_up, w_down, g2, b2):
    t = x2d.shape[0]
    return pl.pallas_call(
        _mix_ffn_kernel,
        out_shape=jax.ShapeDtypeStruct((t, D_MODEL), F32),
        grid=(t // ROW_TILE,),
        in_specs=[
            pl.BlockSpec((ROW_TILE, ATTN_WIDTH), lambda i: (i, 0)),
            pl.BlockSpec((ROW_TILE, SSD_INNER), lambda i: (i, 0)),
            pl.BlockSpec((ROW_TILE, D_MODEL), lambda i: (i, 0)),
            _resident((ATTN_WIDTH + SSD_INNER, D_MODEL)),
            _resident((1, D_MODEL)),
            _resident((1, D_MODEL)),
        ] + _ffn_specs(w_gate.shape[-1], layer),
        out_specs=pl.BlockSpec((ROW_TILE, D_MODEL), lambda i: (i, 0)),
        compiler_params=_params("parallel"),
        name="outproj_ffn_ln",
    )(attn2d, y2d, x2d, w_out, g1, b1, w_gate, w_up, w_down, g2, b2)


def _conformer_ffn_kernel(x_ref, w1_ref, b1_ref, dww_ref, dwb_ref, cg_ref, cb_ref, w2_ref, b2_ref,
                          g_ref, b_ref, wg_ref, wu_ref, wd_ref, g2_ref, b2f_ref, o_ref,
                          hpad_ref, cv_ref, x3_ref, *, tiles_per_seq):
    t = pl.program_id(0)
    ts = CONV_SEQ_TILE
    n_slabs = D_MODEL // LANES

    @pl.when(t == 0)
    def _():
        x3_ref[...] = jnp.zeros(x3_ref.shape, F32)

    @pl.when(t % tiles_per_seq == 0)
    def _():
        hpad_ref[:, 0:CONV_HALO, :] = jnp.zeros((n_slabs, CONV_HALO, LANES), F32)

    xin = x3_ref[(t + 1) % 2]
    xin_b = xin.astype(BF16)

    x = x_ref[...]
    xb = x.astype(BF16)
    val = _dot(xb, w1_ref[:, :D_MODEL]) + b1_ref[:, :D_MODEL]
    gate = _dot(xb, w1_ref[:, D_MODEL:]) + b1_ref[:, D_MODEL:]
    h = val * jax.nn.sigmoid(gate)
    for cc in range(n_slabs):
        hpad_ref[cc, CONV_HALO:CONV_HALO + ts, :] = h[:, cc * LANES:(cc + 1) * LANES]

    tiles = CONV_ROW_CHUNK // SUBLANES

    hid = wg_ref.shape[-1]

    def ffn_piece(c0):
        c1 = min(c0 + FFN_HID_PIECE, hid)
        act = _silu(_dot(xin_b, wg_ref[:, c0:c1])) * _dot(xin_b, wu_ref[:, c0:c1])
        return _dot(act.astype(BF16), wd_ref[c0:c1, :])

    for cc in range(n_slabs):
        cs = slice(cc * LANES, (cc + 1) * LANES)
        taps = [jnp.broadcast_to(dww_ref[k:k + 1, cs], (SUBLANES, LANES)) for k in range(CONF_KERNEL)]
        bias = jnp.broadcast_to(dwb_ref[:, cs], (SUBLANES, LANES))
        for r0 in range(0, ts, CONV_ROW_CHUNK):
            acc = jnp.broadcast_to(bias[None], (tiles, SUBLANES, LANES))
            for k in range(CONF_KERNEL):
                lo = r0 + CONV_HALO - (CONF_KERNEL - 1 - k)
                src = hpad_ref[cc, lo:lo + CONV_ROW_CHUNK, :]
                acc = acc + taps[k][None] * src.reshape(tiles, SUBLANES, LANES)
            cv_ref[cc, r0:r0 + CONV_ROW_CHUNK, :] = acc.reshape(CONV_ROW_CHUNK, LANES)
    hpad_ref[:, 0:CONV_HALO, :] = hpad_ref[:, ts:ts + CONV_HALO, :]

    ffn = ffn_piece(0)
    conv = jnp.concatenate([cv_ref[cc] for cc in range(n_slabs)], axis=1)
    hn = _silu(_layer_norm(conv, cg_ref[...], cb_ref[...]))
    ffn = ffn + ffn_piece(FFN_HID_PIECE)
    mixed = _dot(hn.astype(BF16), w2_ref[...]) + b2_ref[...]
    ffn = ffn + ffn_piece(2 * FFN_HID_PIECE)
    x3_ref[t % 2] = _layer_norm(DN_ALPHA * x + mixed, g_ref[...], b_ref[...])
    o_ref[...] = _layer_norm(DN_ALPHA * xin + ffn, g2_ref[...], b2f_ref[...])


def _conformer_ffn(x2d, seq, w1, b1, dw_w, dw_b, cg, cb, w2, b2, g, b, layer, w_gate, w_up, w_down,
                   g2, b2f):
    t = x2d.shape[0]
    n_tiles = t // CONV_SEQ_TILE
    assert 2 * FFN_HID_PIECE < w_gate.shape[-1] <= 3 * FFN_HID_PIECE
    vec = _resident((1, D_MODEL))
    rows = (CONV_SEQ_TILE, D_MODEL)
    return pl.pallas_call(
        functools.partial(_conformer_ffn_kernel, tiles_per_seq=seq // CONV_SEQ_TILE),
        out_shape=jax.ShapeDtypeStruct((t, D_MODEL), F32),
        grid=(n_tiles + 1,),
        in_specs=[
            pl.BlockSpec(rows, lambda i: (jnp.minimum(i, n_tiles - 1), 0)),
            _resident((D_MODEL, 2 * D_MODEL)),
            _resident((1, 2 * D_MODEL)),
            _resident((CONV_HALO, D_MODEL)),
            vec, vec, vec,
            _resident((D_MODEL, D_MODEL)),
            vec, vec, vec,
        ] + _ffn_specs(w_gate.shape[-1], layer),
        out_specs=pl.BlockSpec(rows, lambda i: (jnp.maximum(i - 1, 0), 0)),
        scratch_shapes=[
            pltpu.VMEM((D_MODEL // LANES, CONV_HALO + CONV_SEQ_TILE, LANES), F32),
            pltpu.VMEM((D_MODEL // LANES, CONV_SEQ_TILE, LANES), F32),
            pltpu.VMEM((2,) + rows, F32),
        ],
        compiler_params=_params("arbitrary"),
        name="conformer_ffn_ln",
    )(x2d, w1, b1, dw_w, dw_b, cg, cb, w2, b2, g, b, w_gate, w_up, w_down, g2, b2f)


def _row(v):
    return v.reshape(1, -1).astype(F32)


def _pad_lanes(v):
    return jnp.pad(v.reshape(1, -1).astype(F32), ((0, 0), (0, LANES - v.shape[-1])))


def kernel(x, mix_w_in, ssd_conv_w, ssd_conv_b, ssd_dt_bias, ssd_a_log, ssd_d, ssd_norm_w, mix_w_out, conv_w_pw1, conv_b_pw1, conv_dw_w, conv_dw_b, conv_ln_g, conv_ln_b, conv_w_pw2, conv_b_pw2, ffn_w_gate, ffn_w_up, ffn_w_down, ln_mix_g, ln_mix_b, ln_ffn_g, ln_ffn_b):
    bsz, seq, d = x.shape
    t = bsz * seq
    n_heads = ATTN_WIDTH // HEAD_DIM
    slopes = 2.0 ** (-8.0 * jnp.arange(1, n_heads + 1, dtype=F32) / n_heads)

    w_in = jnp.pad(mix_w_in[0], ((0, 0), (0, IN_COLS_PAD - mix_w_in.shape[-1]))).astype(BF16)
    h = _inproj(x.reshape(t, d), w_in).reshape(bsz, seq, IN_COLS_PAD)
    attn = _moba(h, slopes)
    y = _ssd(h, ssd_conv_w[0], _row(ssd_conv_b[0]), _pad_lanes(ssd_dt_bias[0]), _pad_lanes(ssd_a_log[0]),
             _row(jnp.repeat(ssd_d[0], HEAD_DIM)), _row(ssd_norm_w[0]))
    ffn_w = (ffn_w_gate.astype(BF16), ffn_w_up.astype(BF16), ffn_w_down.astype(BF16))
    x2 = _mix_ffn(attn.reshape(t, ATTN_WIDTH), y.reshape(t, SSD_INNER), x.reshape(t, d),
                  mix_w_out[0].astype(BF16), _row(ln_mix_g[0]), _row(ln_mix_b[0]),
                  0, *ffn_w, _row(ln_ffn_g[0]), _row(ln_ffn_b[0]))

    dw_w = jnp.pad(conv_dw_w[0], ((0, CONV_HALO - CONF_KERNEL), (0, 0)))
    x4 = _conformer_ffn(x2, seq, conv_w_pw1[0].astype(BF16), _row(conv_b_pw1[0]),
                        dw_w, _row(conv_dw_b[0]), _row(conv_ln_g[0]), _row(conv_ln_b[0]),
                        conv_w_pw2[0].astype(BF16), _row(conv_b_pw2[0]),
                        _row(ln_mix_g[1]), _row(ln_mix_b[1]),
                        1, *ffn_w, _row(ln_ffn_g[1]), _row(ln_ffn_b[1]))
    return x4.reshape(bsz, seq, d)
```

```python
import functools

import jax
import jax.numpy as jnp
from jax import lax
from jax.experimental import pallas as pl
from jax.experimental.pallas import tpu as pltpu

F32 = jnp.float32
BF16 = jnp.bfloat16

D_MODEL = 1024
DEPTH = 2
ATTN_WIDTH = 512
HEAD_DIM = 64
MOBA_BLOCK = 256
MOBA_TOPK = 3
SSD_INNER = 512
SSD_HEADS = 8
SSD_GROUPS = 2
SSD_STATE = 128
SSD_CONV = 4
SSD_CHUNK = 256
SSD_CONV_CH = SSD_INNER + 2 * SSD_GROUPS * SSD_STATE
CONF_KERNEL = 31
DN_ALPHA = (2 * DEPTH) ** 0.25
LN_EPS = 1e-5
RMS_EPS = 1e-5

LANES = 128
SUBLANES = 8

COL_Q = 0
COL_K = ATTN_WIDTH
COL_V = 2 * ATTN_WIDTH
COL_Z = 3 * ATTN_WIDTH
COL_XBC = COL_Z + SSD_INNER
COL_DT = COL_XBC + SSD_CONV_CH
IN_COLS_PAD = COL_DT + LANES

MASK_NEG = -1e30
LOG2_E = 1.4426950408889634

ROW_TILE = 512
INPROJ_COL_CHUNK = 512
CONV_SEQ_TILE = 512
CONV_HALO = 32
CONV_ROW_CHUNK = 128
FFN_HID_PIECE = 1024
MOBA_SKEW = 2
MOBA_SUM_ROWS = 16
SSD_HALO = 8
VMEM_LIMIT = 56 * 1024 * 1024


def _params(*sem):
    return pltpu.CompilerParams(dimension_semantics=sem, vmem_limit_bytes=VMEM_LIMIT)


def _dot(a, b):
    return jnp.dot(a, b, preferred_element_type=F32)


def _dot_nt(a, b):
    return lax.dot_general(a, b, (((1,), (1,)), ((), ())), preferred_element_type=F32)


def _split3(x):
    hi = x.astype(BF16)
    r1 = x - hi.astype(F32)
    mid = r1.astype(BF16)
    lo = (r1 - mid.astype(F32)).astype(BF16)
    return hi, mid, lo


def _silu(x):
    return x * jax.nn.sigmoid(x)


def _layer_norm(r, g, b):
    mu = jnp.mean(r, axis=-1, keepdims=True)
    d = r - mu
    var = jnp.mean(d * d, axis=-1, keepdims=True)
    return d * lax.rsqrt(var + LN_EPS) * g + b


def _inproj_kernel(x_ref, w_ref, o_ref):
    xb = x_ref[...].astype(BF16)
    for c0 in range(0, IN_COLS_PAD, INPROJ_COL_CHUNK):
        c1 = min(c0 + INPROJ_COL_CHUNK, IN_COLS_PAD)
        o_ref[:, c0:c1] = _dot(xb, w_ref[:, c0:c1])


def _resident(shape):
    return pl.BlockSpec(shape, lambda *_: (0,) * len(shape), pipeline_mode=pl.Buffered(1))


def _inproj(x2d, w_pad):
    t = x2d.shape[0]
    return pl.pallas_call(
        _inproj_kernel,
        out_shape=jax.ShapeDtypeStruct((t, IN_COLS_PAD), F32),
        grid=(t // ROW_TILE,),
        in_specs=[
            pl.BlockSpec((ROW_TILE, D_MODEL), lambda i: (i, 0)),
            _resident((D_MODEL, IN_COLS_PAD)),
        ],
        out_specs=pl.BlockSpec((ROW_TILE, IN_COLS_PAD), lambda i: (i, 0)),
        compiler_params=_params("parallel"),
        name="inproj",
    )(x2d, w_pad)


def _moba_kernel(slopes_ref, q_ref, k_ref, v_ref, o_ref, kb_ref, vt_ref, km_ref, al_ref, s_ref):
    hp = pl.program_id(1)
    nb = kb_ref.shape[0]
    blk = MOBA_BLOCK
    scale = HEAD_DIM ** -0.5 * LOG2_E

    for n in range(nb):
        kblk = k_ref[0, n * blk:(n + 1) * blk, :]
        kb_ref[n] = kblk.astype(BF16)
        km_ref[n:n + 1, :] = jnp.sum(kblk, axis=0, keepdims=True) * (1.0 / blk)
        vt = v_ref[0, n * blk:(n + 1) * blk, :].T.astype(BF16)
        for hd in range(2):
            vt_ref[n, hd, 0:HEAD_DIM, :] = vt[hd * HEAD_DIM:(hd + 1) * HEAD_DIM, :]
            vt_ref[n, hd, HEAD_DIM:, :] = jnp.ones((MOBA_SUM_ROWS, blk), BF16)

    lane = lax.broadcasted_iota(jnp.int32, (blk, LANES), 1)
    key_in = lax.broadcasted_iota(jnp.int32, (blk, blk), 0)
    qry_in = lax.broadcasted_iota(jnp.int32, (blk, blk), 1)
    dist0 = (qry_in - key_in).astype(F32)
    causal = qry_in >= key_in
    km = km_ref[...]
    km_hi = km.astype(BF16)
    km_lo = (km - km_hi.astype(F32)).astype(BF16)
    head_lanes = [(lane >= hd * HEAD_DIM) & (lane < (hd + 1) * HEAD_DIM) for hd in range(2)]
    slopes = [slopes_ref[2 * hp + hd] * LOG2_E for hd in range(2)]
    for hd in range(2):
        al = -slopes[hd] * dist0
        al_ref[hd, 0] = al
        al_ref[hd, 1] = jnp.where(causal, al, MASK_NEG)

    def scores(i, hd):
        slope = slopes[hd]
        q = q_ref[0, i * blk:(i + 1) * blk, :]
        q_f = jnp.where(head_lanes[hd], q, 0.0) * scale
        q_hi = q_f.astype(BF16)
        if i > MOBA_TOPK:
            q_lo = (q_f - q_hi.astype(F32)).astype(BF16)
            gate = (_dot_nt(km_hi, q_hi) + _dot_nt(km_lo, q_hi)
                    + _dot_nt(km_hi, q_lo))
            blk_id = lax.broadcasted_iota(jnp.int32, gate.shape, 0)
            rank = jnp.zeros(gate.shape, jnp.int32)
            for m in range(i):
                gm = gate[m:m + 1, :]
                beats = (gm > gate) | ((gm == gate) & (blk_id > m))
                rank = rank + jnp.where(beats & (blk_id != m), 1, 0)
            blk_dist = ((i - blk_id) * blk).astype(F32)
            rb = jnp.where(rank < MOBA_TOPK, -slope * blk_dist, MASK_NEG)
            row_bias = [rb[j:j + 1, :] for j in range(i)]
        else:
            row_bias = [-slope * float((i - j) * blk) for j in range(i)]
        sbuf = s_ref.at[i % 2, hd]
        s = _dot_nt(kb_ref[i], q_hi) + al_ref[hd, 1]
        sbuf[i] = s
        m_max = jnp.max(s, axis=0, keepdims=True)
        for j in range(i):
            s = _dot_nt(kb_ref[j], q_hi) + al_ref[hd, 0] + row_bias[j]
            sbuf[j] = s
            m_max = jnp.maximum(m_max, jnp.max(s, axis=0, keepdims=True))
        return m_max

    def values(i, hd, m_max):
        sbuf = s_ref.at[i % 2, hd]
        acc = jnp.zeros((HEAD_DIM + MOBA_SUM_ROWS, blk), F32)
        for j in range(i + 1):
            p = jnp.exp2(sbuf[j] - m_max)
            acc = acc + _dot(vt_ref[j, hd], p.astype(BF16))
        return acc[0:HEAD_DIM] / acc[HEAD_DIM:HEAD_DIM + 1]

    units = [(i, hd) for i in range(nb) for hd in range(2)]
    outs = {}
    maxima = [scores(*unit) for unit in units[:MOBA_SKEW]]
    for u, (i, hd) in enumerate(units):
        if u + MOBA_SKEW < len(units):
            maxima.append(scores(*units[u + MOBA_SKEW]))
        outs[hd] = values(i, hd, maxima[u])
        if hd == 1:
            o_ref[0, i * blk:(i + 1) * blk, :] = jnp.concatenate([outs[0], outs[1]], axis=0).T


def _moba(h3d, slopes):
    bsz, seq, _ = h3d.shape
    nb = seq // MOBA_BLOCK
    n_pairs = ATTN_WIDTH // LANES
    return pl.pallas_call(
        _moba_kernel,
        out_shape=jax.ShapeDtypeStruct((bsz, seq, ATTN_WIDTH), F32),
        grid=(bsz, n_pairs),
        in_specs=[
            pl.BlockSpec(memory_space=pltpu.SMEM),
            pl.BlockSpec((1, seq, LANES), lambda b, p: (b, 0, COL_Q // LANES + p)),
            pl.BlockSpec((1, seq, LANES), lambda b, p: (b, 0, COL_K // LANES + p)),
            pl.BlockSpec((1, seq, LANES), lambda b, p: (b, 0, COL_V // LANES + p)),
        ],
        out_specs=pl.BlockSpec((1, seq, LANES), lambda b, p: (b, 0, p)),
        scratch_shapes=[
            pltpu.VMEM((nb, MOBA_BLOCK, LANES), BF16),
            pltpu.VMEM((nb, 2, HEAD_DIM + MOBA_SUM_ROWS, MOBA_BLOCK), BF16),
            pltpu.VMEM((nb, LANES), F32),
            pltpu.VMEM((2, 2, MOBA_BLOCK, MOBA_BLOCK), F32),
            pltpu.VMEM((2, 2, nb, MOBA_BLOCK, MOBA_BLOCK), F32),
        ],
        compiler_params=_params("parallel", "parallel"),
        name="moba_attention",
    )(slopes, h3d, h3d, h3d)


def _ssd_kernel(xbc_ref, dt_ref, z_ref, cw_ref, cb_ref, dtb_ref, alog_ref, dskip_ref, nw_ref,
                y_ref, xpad_ref, hst_ref):
    c = pl.program_id(1)
    L = SSD_CHUNK
    gw = SSD_INNER // SSD_GROUPS

    @pl.when(c == 0)
    def _():
        xpad_ref[:, 0:SSD_HALO, :] = jnp.zeros((SSD_CONV_CH // LANES, SSD_HALO, LANES), F32)
        hst_ref[...] = jnp.zeros(hst_ref.shape, F32)

    conv_slabs = []
    for cc in range(SSD_CONV_CH // LANES):
        cs = slice(cc * LANES, (cc + 1) * LANES)
        xpad_ref[cc, SSD_HALO:SSD_HALO + L, :] = xbc_ref[0, :, cs]
        acc = jnp.broadcast_to(cb_ref[:, cs], (L, LANES))
        for k in range(SSD_CONV):
            off = SSD_HALO - (SSD_CONV - 1) + k
            acc = acc + cw_ref[k:k + 1, cs] * xpad_ref[cc, off:off + L, :]
        conv_slabs.append(acc)
    xpad_ref[:, 0:SSD_HALO, :] = xpad_ref[:, L:L + SSD_HALO, :]
    conv = jnp.concatenate(conv_slabs, axis=1)
    xbc = _silu(conv)
    xs = xbc[:, :SSD_INNER]

    dt_in = dt_ref[0] + dtb_ref[...]
    dt = jnp.maximum(dt_in, 0.0) + jnp.log1p(jnp.exp(-jnp.abs(dt_in)))
    a = -jnp.exp(alog_ref[...])
    a_dt = dt * a
    row = lax.broadcasted_iota(jnp.int32, (L, L), 0)
    col = lax.broadcasted_iota(jnp.int32, (L, L), 1)
    causal = row >= col
    tril = jnp.where(causal, 1.0, 0.0).astype(BF16)
    a_cs = _dot_exact_rhs_lhs(tril, a_dt)
    a_cs_t = a_cs.T
    a_last = a_cs[L - 1:L, :]
    decay_to_end = jnp.exp(a_last - a_cs)
    decay_from_start = jnp.exp(a_cs)

    eh = lax.broadcasted_iota(jnp.int32, (2 * LANES, SSD_INNER), 0) & (LANES - 1)
    ec = lax.broadcasted_iota(jnp.int32, (2 * LANES, SSD_INNER), 1)
    expand = jnp.where((ec >= eh * HEAD_DIM) & (ec < (eh + 1) * HEAD_DIM), 1.0, 0.0).astype(BF16)
    per_head = jnp.concatenate([dt, decay_to_end, decay_from_start], axis=0)
    ph_hi = per_head.astype(BF16)
    ph_lo = (per_head - ph_hi.astype(F32)).astype(BF16)
    per_chan = _dot(jnp.concatenate([ph_hi, ph_lo], axis=1), expand)
    dt_full = per_chan[0:L]
    dte_full = per_chan[L:2 * L]
    dfs_full = per_chan[2 * L:3 * L]

    x_dt = xs * dt_full
    x_dte = (x_dt * dte_full).astype(BF16)
    x_dt_b = x_dt.astype(BF16)
    first_head = lax.broadcasted_iota(jnp.int32, (L, LANES), 1) < HEAD_DIM
    heads_per_group = SSD_HEADS // SSD_GROUPS

    for g in range(SSD_GROUPS):
        bm = xbc[:, SSD_INNER + g * SSD_STATE:SSD_INNER + (g + 1) * SSD_STATE]
        cm = xbc[:, SSD_INNER + (SSD_GROUPS + g) * SSD_STATE:SSD_INNER + (SSD_GROUPS + g + 1) * SSD_STATE]
        bm_b = bm.astype(BF16)
        cm_b = cm.astype(BF16)
        cbm = _dot_nt(cm_b, bm_b)
        gs = slice(g * gw, (g + 1) * gw)

        h_prev = hst_ref[g]
        y_g = _dot(cm_b, h_prev.astype(BF16)) * dfs_full[:, gs]
        st = _dot(bm.T.astype(BF16), x_dte[:, gs])
        hst_ref[g] = h_prev * dfs_full[L - 1:L, gs] + st

        pieces = []
        for pi in range(heads_per_group // 2):
            ls = slice(g * gw + pi * LANES, g * gw + (pi + 1) * LANES)
            xp = x_dt_b[:, ls]
            yd = []
            for hd in range(2):
                h = g * heads_per_group + 2 * pi + hd
                halves = []
                for rows, cols in ((slice(0, L // 2), slice(0, L // 2)), (slice(L // 2, L), slice(0, L))):
                    seg = a_cs[rows, h:h + 1] - a_cs_t[h:h + 1, cols]
                    dec = jnp.exp(jnp.where(causal[rows, cols], seg, MASK_NEG))
                    halves.append(_dot((cbm[rows, cols] * dec).astype(BF16), xp[cols, :]))
                yd.append(jnp.concatenate(halves, axis=0))
            pieces.append(jnp.where(first_head, yd[0], yd[1]))
        y_g = y_g + jnp.concatenate(pieces, axis=1) + xs[:, gs] * dskip_ref[:, gs]

        gz = y_g * _silu(z_ref[0, :, gs])
        ms = jnp.mean(gz * gz, axis=-1, keepdims=True)
        y_ref[0, :, gs] = gz * lax.rsqrt(ms + RMS_EPS) * nw_ref[:, gs]


def _dot_exact_rhs_lhs(lhs_bf16, x):
    hi, mid, lo = _split3(x)
    return _dot(lhs_bf16, hi) + _dot(lhs_bf16, mid) + _dot(lhs_bf16, lo)


def _ssd(h3d, conv_w, conv_b, dt_bias, a_log, d_full, norm_w):
    bsz, seq, _ = h3d.shape
    n_c = seq // SSD_CHUNK
    const = lambda b, c: (0, 0)
    return pl.pallas_call(
        _ssd_kernel,
        out_shape=jax.ShapeDtypeStruct((bsz, seq, SSD_INNER), F32),
        grid=(bsz, n_c),
        in_specs=[
            pl.BlockSpec((1, SSD_CHUNK, SSD_CONV_CH), lambda b, c: (b, c, COL_XBC // SSD_CONV_CH)),
            pl.BlockSpec((1, SSD_CHUNK, LANES), lambda b, c: (b, c, COL_DT // LANES)),
            pl.BlockSpec((1, SSD_CHUNK, SSD_INNER), lambda b, c: (b, c, COL_Z // SSD_INNER)),
            pl.BlockSpec((SSD_CONV, SSD_CONV_CH), const),
            pl.BlockSpec((1, SSD_CONV_CH), const),
            pl.BlockSpec((1, LANES), const),
            pl.BlockSpec((1, LANES), const),
            pl.BlockSpec((1, SSD_INNER), const),
            pl.BlockSpec((1, SSD_INNER), const),
        ],
        out_specs=pl.BlockSpec((1, SSD_CHUNK, SSD_INNER), lambda b, c: (b, c, 0)),
        scratch_shapes=[
            pltpu.VMEM((SSD_CONV_CH // LANES, SSD_HALO + SSD_CHUNK, LANES), F32),
            pltpu.VMEM((SSD_GROUPS, SSD_STATE, SSD_INNER // SSD_GROUPS), F32),
        ],
        compiler_params=_params("parallel", "arbitrary"),
        name="ssd_scan",
    )(h3d, h3d, h3d, conv_w, conv_b, dt_bias, a_log, d_full, norm_w)


def _ffn_ln_value(x, wg_ref, wu_ref, wd_ref, g_ref, b_ref):
    xb = x.astype(BF16)
    gate = _dot(xb, wg_ref[...])
    up = _dot(xb, wu_ref[...])
    ffn = _dot((_silu(gate) * up).astype(BF16), wd_ref[...])
    return _layer_norm(DN_ALPHA * x + ffn, g_ref[...], b_ref[...])


def _ffn_specs(hid, layer):
    def stacked(rows, cols):
        return pl.BlockSpec((None, rows, cols), lambda *_: (layer, 0, 0), pipeline_mode=pl.Buffered(1))
    return [stacked(D_MODEL, hid), stacked(D_MODEL, hid), stacked(hid, D_MODEL),
            _resident((1, D_MODEL)), _resident((1, D_MODEL))]


def _mix_ffn_kernel(attn_ref, y_ref, x_ref, wo_ref, g1_ref, b1_ref,
                    wg_ref, wu_ref, wd_ref, g2_ref, b2_ref, o_ref):
    half = ROW_TILE // 2
    rows = [slice(0, half), slice(half, ROW_TILE)]

    def mixed(rs):
        return (_dot(attn_ref[rs, :].astype(BF16), wo_ref[0:ATTN_WIDTH, :])
                + _dot(y_ref[rs, :].astype(BF16), wo_ref[ATTN_WIDTH:, :]))

    def norm1(rs, m):
        return _layer_norm(DN_ALPHA * x_ref[rs, :] + m, g1_ref[...], b1_ref[...])

    def act(x1):
        xb = x1.astype(BF16)
        return (_silu(_dot(xb, wg_ref[...])) * _dot(xb, wu_ref[...])).astype(BF16)

    def norm2(x1, a):
        return _layer_norm(DN_ALPHA * x1 + _dot(a, wd_ref[...]), g2_ref[...], b2_ref[...])

    m_a, m_b = mixed(rows[0]), mixed(rows[1])
    x1_a = norm1(rows[0], m_a)
    act_a = act(x1_a)
    x1_b = norm1(rows[1], m_b)
    act_b = act(x1_b)
    o_ref[rows[0], :] = norm2(x1_a, act_a)
    o_ref[rows[1], :] = norm2(x1_b, act_b)


def _mix_ffn(attn2d, y2d, x2d, w_out, g1, b1, layer, w_gate, w_up, w_down, g2, b2):
    t = x2d.shape[0]
    return pl.pallas_call(
        _mix_ffn_kernel,
        out_shape=jax.ShapeDtypeStruct((t, D_MODEL), F32),
        grid=(t // ROW_TILE,),
        in_specs=[
            pl.BlockSpec((ROW_TILE, ATTN_WIDTH), lambda i: (i, 0)),
            pl.BlockSpec((ROW_TILE, SSD_INNER), lambda i: (i, 0)),
            pl.BlockSpec((ROW_TILE, D_MODEL), lambda i: (i, 0)),
            _resident((ATTN_WIDTH + SSD_INNER, D_MODEL)),
            _resident((1, D_MODEL)),
            _resident((1, D_MODEL)),
        ] + _ffn_specs(w_gate.shape[-1], layer),
        out_specs=pl.BlockSpec((ROW_TILE, D_MODEL), lambda i: (i, 0)),
        compiler_params=_params("parallel"),
        name="outproj_ffn_ln",
    )(attn2d, y2d, x2d, w_out, g1, b1, w_gate, w_up, w_down, g2, b2)


def _conformer_ffn_kernel(x_ref, w1_ref, b1_ref, dww_ref, dwb_ref, cg_ref, cb_ref, w2_ref, b2_ref,
                          g_ref, b_ref, wg_ref, wu_ref, wd_ref, g2_ref, b2f_ref, o_ref,
                          hpad_ref, cv_ref, x3_ref, *, tiles_per_seq):
    t = pl.program_id(0)
    ts = CONV_SEQ_TILE
    n_slabs = D_MODEL // LANES

    @pl.when(t == 0)
    def _():
        x3_ref[...] = jnp.zeros(x3_ref.shape, F32)

    @pl.when(t % tiles_per_seq == 0)
    def _():
        hpad_ref[:, 0:CONV_HALO, :] = jnp.zeros((n_slabs, CONV_HALO, LANES), F32)

    xin = x3_ref[(t + 1) % 2]
    xin_b = xin.astype(BF16)

    x = x_ref[...]
    xb = x.astype(BF16)
    val = _dot(xb, w1_ref[:, :D_MODEL]) + b1_ref[:, :D_MODEL]
    gate = _dot(xb, w1_ref[:, D_MODEL:]) + b1_ref[:, D_MODEL:]
    h = val * jax.nn.sigmoid(gate)
    for cc in range(n_slabs):
        hpad_ref[cc, CONV_HALO:CONV_HALO + ts, :] = h[:, cc * LANES:(cc + 1) * LANES]

    tiles = CONV_ROW_CHUNK // SUBLANES

    hid = wg_ref.shape[-1]

    def ffn_piece(c0):
        c1 = min(c0 + FFN_HID_PIECE, hid)
        act = _silu(_dot(xin_b, wg_ref[:, c0:c1])) * _dot(xin_b, wu_ref[:, c0:c1])
        return _dot(act.astype(BF16), wd_ref[c0:c1, :])

    for cc in range(n_slabs):
        cs = slice(cc * LANES, (cc + 1) * LANES)
        taps = [jnp.broadcast_to(dww_ref[k:k + 1, cs], (SUBLANES, LANES)) for k in range(CONF_KERNEL)]
        bias = jnp.broadcast_to(dwb_ref[:, cs], (SUBLANES, LANES))
        for r0 in range(0, ts, CONV_ROW_CHUNK):
            acc = jnp.broadcast_to(bias[None], (tiles, SUBLANES, LANES))
            for k in range(CONF_KERNEL):
                lo = r0 + CONV_HALO - (CONF_KERNEL - 1 - k)
                src = hpad_ref[cc, lo:lo + CONV_ROW_CHUNK, :]
                acc = acc + taps[k][None] * src.reshape(tiles, SUBLANES, LANES)
            cv_ref[cc, r0:r0 + CONV_ROW_CHUNK, :] = acc.reshape(CONV_ROW_CHUNK, LANES)
    hpad_ref[:, 0:CONV_HALO, :] = hpad_ref[:, ts:ts + CONV_HALO, :]

    ffn = ffn_piece(0)
    conv = jnp.concatenate([cv_ref[cc] for cc in range(n_slabs)], axis=1)
    hn = _silu(_layer_norm(conv, cg_ref[...], cb_ref[...]))
    ffn = ffn + ffn_piece(FFN_HID_PIECE)
    mixed = _dot(hn.astype(BF16), w2_ref[...]) + b2_ref[...]
    ffn = ffn + ffn_piece(2 * FFN_HID_PIECE)
    x3_ref[t % 2] = _layer_norm(DN_ALPHA * x + mixed, g_ref[...], b_ref[...])
    o_ref[...] = _layer_norm(DN_ALPHA * xin + ffn, g2_ref[...], b2f_ref[...])


def _conformer_ffn(x2d, seq, w1, b1, dw_w, dw_b, cg, cb, w2, b2, g, b, layer, w_gate, w_up, w_down,
                   g2, b2f):
    t = x2d.shape[0]
    n_tiles = t // CONV_SEQ_TILE
    assert 2 * FFN_HID_PIECE < w_gate.shape[-1] <= 3 * FFN_HID_PIECE
    vec = _resident((1, D_MODEL))
    rows = (CONV_SEQ_TILE, D_MODEL)
    return pl.pallas_call(
        functools.partial(_conformer_ffn_kernel, tiles_per_seq=seq // CONV_SEQ_TILE),
        out_shape=jax.ShapeDtypeStruct((t, D_MODEL), F32),
        grid=(n_tiles + 1,),
        in_specs=[
            pl.BlockSpec(rows, lambda i: (jnp.minimum(i, n_tiles - 1), 0)),
            _resident((D_MODEL, 2 * D_MODEL)),
            _resident((1, 2 * D_MODEL)),
            _resident((CONV_HALO, D_MODEL)),
            vec, vec, vec,
            _resident((D_MODEL, D_MODEL)),
            vec, vec, vec,
        ] + _ffn_specs(w_gate.shape[-1], layer),
        out_specs=pl.BlockSpec(rows, lambda i: (jnp.maximum(i - 1, 0), 0)),
        scratch_shapes=[
            pltpu.VMEM((D_MODEL // LANES, CONV_HALO + CONV_SEQ_TILE, LANES), F32),
            pltpu.VMEM((D_MODEL // LANES, CONV_SEQ_TILE, LANES), F32),
            pltpu.VMEM((2,) + rows, F32),
        ],
        compiler_params=_params("arbitrary"),
        name="conformer_ffn_ln",
    )(x2d, w1, b1, dw_w, dw_b, cg, cb, w2, b2, g, b, w_gate, w_up, w_down, g2, b2f)


def _row(v):
    return v.reshape(1, -1).astype(F32)


def _pad_lanes(v):
    return jnp.pad(v.reshape(1, -1).astype(F32), ((0, 0), (0, LANES - v.shape[-1])))


def kernel(x, mix_w_in, ssd_conv_w, ssd_conv_b, ssd_dt_bias, ssd_a_log, ssd_d, ssd_norm_w, mix_w_out, conv_w_pw1, conv_b_pw1, conv_dw_w, conv_dw_b, conv_ln_g, conv_ln_b, conv_w_pw2, conv_b_pw2, ffn_w_gate, ffn_w_up, ffn_w_down, ln_mix_g, ln_mix_b, ln_ffn_g, ln_ffn_b):
    bsz, seq, d = x.shape
    t = bsz * seq
    n_heads = ATTN_WIDTH // HEAD_DIM
    slopes = 2.0 ** (-8.0 * jnp.arange(1, n_heads + 1, dtype=F32) / n_heads)

    w_in = jnp.pad(mix_w_in[0], ((0, 0), (0, IN_COLS_PAD - mix_w_in.shape[-1]))).astype(BF16)
    h = _inproj(x.reshape(t, d), w_in).reshape(bsz, seq, IN_COLS_PAD)
    attn = _moba(h, slopes)
    y = _ssd(h, ssd_conv_w[0], _row(ssd_conv_b[0]), _pad_lanes(ssd_dt_bias[0]), _pad_lanes(ssd_a_log[0]),
             _row(jnp.repeat(ssd_d[0], HEAD_DIM)), _row(ssd_norm_w[0]))
    ffn_w = (ffn_w_gate.astype(BF16), ffn_w_up.astype(BF16), ffn_w_down.astype(BF16))
    x2 = _mix_ffn(attn.reshape(t, ATTN_WIDTH), y.reshape(t, SSD_INNER), x.reshape(t, d),
                  mix_w_out[0].astype(BF16), _row(ln_mix_g[0]), _row(ln_mix_b[0]),
                  0, *ffn_w, _row(ln_ffn_g[0]), _row(ln_ffn_b[0]))

    dw_w = jnp.pad(conv_dw_w[0], ((0, CONV_HALO - CONF_KERNEL), (0, 0)))
    x4 = _conformer_ffn(x2, seq, conv_w_pw1[0].astype(BF16), _row(conv_b_pw1[0]),
                        dw_w, _row(conv_dw_b[0]), _row(conv_ln_g[0]), _row(conv_ln_b[0]),
                        conv_w_pw2[0].astype(BF16), _row(conv_b_pw2[0]),
                        _row(ln_mix_g[1]), _row(ln_mix_b[1]),
                        1, *ffn_w, _row(ln_ffn_g[1]), _row(ln_ffn_b[1]))
    return x4.reshape(bsz, seq, d)
```
